```python
import math
import jax, jax.numpy as jnp
from jax import lax
import numpy as np

D_MODEL = 1024
BATCH = 4
SEQ = 4096
DEPTH = 1

HEAD_DIM = 64
N_HEADS = D_MODEL // HEAD_DIM
N_DIFF_HEADS = N_HEADS // 4
DIFF_WIDTH = N_DIFF_HEADS * 2 * HEAD_DIM
N_MOBA_HEADS = N_HEADS // 2
MOBA_WIDTH = N_MOBA_HEADS * HEAD_DIM
MIX_WIDTH = DIFF_WIDTH + MOBA_WIDTH
ROT_DIM = HEAD_DIM // 4
ROPE_THETA = 500000.0
MOBA_BLOCK = 256
MOBA_TOPK = 3
MOBA_Q_CHUNK = 32
DIFF_Q_BLOCK = 128
D_FF = -(-8 * D_MODEL // (3 * 256)) * 256
ALPHA = (2.0 * DEPTH) ** 0.25
BETA = (8.0 * DEPTH) ** -0.25
LN_EPS = 1e-5

kernel_name = "hymba_diffattn_moba_deepnorm"


def lambda_init_for(layer_idx):
    return 0.8 - 0.6 * math.exp(-0.3 * layer_idx)


def rope_tables(seq):
    inv = ROPE_THETA ** (-jnp.arange(0, ROT_DIM, 2, dtype=jnp.float32) / ROT_DIM)
    ang = jnp.arange(seq, dtype=jnp.float32)[:, None] * inv[None, :]
    return jnp.cos(ang), jnp.sin(ang)


def partial_rope(x, cos, sin):
    half = ROT_DIM // 2
    x1 = x[..., :half].astype(jnp.float32)
    x2 = x[..., half:ROT_DIM].astype(jnp.float32)
    rot = jnp.concatenate([x1 * cos - x2 * sin, x2 * cos + x1 * sin], axis=-1).astype(x.dtype)
    return jnp.concatenate([rot, x[..., ROT_DIM:]], axis=-1)


def layer_norm(x, g, b):
    xf = x.astype(jnp.float32)
    mu = jnp.mean(xf, axis=-1, keepdims=True)
    var = jnp.mean(jnp.square(xf - mu), axis=-1, keepdims=True)
    return ((xf - mu) * lax.rsqrt(var + LN_EPS) * g.astype(jnp.float32) + b.astype(jnp.float32)).astype(x.dtype)


def diff_attention(q, k, v, lam_params, subln_g, lambda_init, cos, sin):
    B, S, _ = q.shape
    H, dh = N_DIFF_HEADS, HEAD_DIM
    q = q.reshape(B, S, H, 2, dh).transpose(0, 2, 3, 1, 4)
    k = k.reshape(B, S, H, 2, dh).transpose(0, 2, 3, 1, 4)
    v = v.reshape(B, S, H, 2 * dh).transpose(0, 2, 1, 3)
    q = partial_rope(q, cos, sin)
    k = partial_rope(k, cos, sin)
    lp = lam_params.astype(jnp.float32)
    lam = jnp.exp(jnp.sum(lp[0] * lp[1])) - jnp.exp(jnp.sum(lp[2] * lp[3])) + lambda_init
    scale = dh ** -0.5
    nblk = S // DIFF_Q_BLOCK
    qb = q.reshape(B, H, 2, nblk, DIFF_Q_BLOCK, dh).transpose(3, 0, 1, 2, 4, 5)
    kpos = jnp.arange(S)

    def one_block(args):
        qblk, i = args
        qpos = i * DIFF_Q_BLOCK + jnp.arange(DIFF_Q_BLOCK)
        s = jnp.einsum('bhcqd,bhckd->bhcqk', qblk, k).astype(jnp.float32) * scale
        s = jnp.where(kpos[None, :] <= qpos[:, None], s, -jnp.inf)
        p = jax.nn.softmax(s, axis=-1)
        a = p[:, :, 0] - lam * p[:, :, 1]
        return jnp.einsum('bhqk,bhkd->bhqd', a.astype(v.dtype), v)

    o = lax.map(one_block, (qb, jnp.arange(nblk)))
    o = o.transpose(1, 2, 0, 3, 4).reshape(B, H, S, 2 * dh)
    of = o.astype(jnp.float32)
    of = of * lax.rsqrt(jnp.mean(jnp.square(of), axis=-1, keepdims=True) + LN_EPS)
    of = of * subln_g.astype(jnp.float32) * (1.0 - lambda_init)
    return of.astype(v.dtype).transpose(0, 2, 1, 3).reshape(B, S, DIFF_WIDTH)


def moba_attention(q, k, v, cos, sin):
    B, S, _ = q.shape
    H, dh, BS, QC = N_MOBA_HEADS, HEAD_DIM, MOBA_BLOCK, MOBA_Q_CHUNK
    q = partial_rope(q.reshape(B, S, H, dh).transpose(0, 2, 1, 3), cos, sin)
    k = partial_rope(k.reshape(B, S, H, dh).transpose(0, 2, 1, 3), cos, sin)
    v = v.reshape(B, S, H, dh).transpose(0, 2, 1, 3)
    nb = -(-S // BS)
    pad = nb * BS - S
    k = jnp.pad(k, ((0, 0), (0, 0), (0, pad), (0, 0)))
    v = jnp.pad(v, ((0, 0), (0, 0), (0, pad), (0, 0)))
    kb = k.reshape(B, H, nb, BS, dh)
    vb = v.reshape(B, H, nb, BS, dh)
    kmean = jnp.mean(kb.astype(jnp.float32), axis=3).astype(k.dtype)
    topk = min(MOBA_TOPK, nb)
    scale = dh ** -0.5
    nc = S // QC
    qc = q.reshape(B, H, nc, QC, dh).transpose(2, 0, 1, 3, 4)
    blk_ids = jnp.arange(nb)
    gather = jax.vmap(jax.vmap(lambda blocks, idx: blocks[idx]))

    def one_chunk(args):
        qch, c = args
        start = c * QC
        qpos = start + jnp.arange(QC)
        own = start // BS
        gate = jnp.einsum('bhqd,bhnd->bhqn', qch, kmean).astype(jnp.float32)
        gate = jnp.where(blk_ids < own, gate, -jnp.inf)
        _, idx = lax.top_k(gate, topk)
        valid = jnp.arange(topk) < own
        k_sel = gather(kb, idx)
        v_sel = gather(vb, idx)
        s_sel = jnp.einsum('bhqd,bhqjkd->bhqjk', qch, k_sel).astype(jnp.float32) * scale
        s_sel = jnp.where(valid[:, None], s_sel, -jnp.inf).reshape(B, H, QC, topk * BS)
        k_own = lax.dynamic_index_in_dim(kb, own, axis=2, keepdims=False)
        v_own = lax.dynamic_index_in_dim(vb, own, axis=2, keepdims=False)
        kpos_own = own * BS + jnp.arange(BS)
        s_own = jnp.einsum('bhqd,bhkd->bhqk', qch, k_own).astype(jnp.float32) * scale
        s_own = jnp.where(kpos_own[None, :] <= qpos[:, None], s_own, -jnp.inf)
        p = jax.nn.softmax(jnp.concatenate([s_sel, s_own], axis=-1), axis=-1).astype(v.dtype)
        p_sel = p[..., :topk * BS].reshape(B, H, QC, topk, BS)
        p_own = p[..., topk * BS:]
        return (jnp.einsum('bhqjk,bhqjkd->bhqd', p_sel, v_sel)
                + jnp.einsum('bhqk,bhkd->bhqd', p_own, v_own))

    o = lax.map(one_chunk, (qc, jnp.arange(nc)))
    return o.transpose(1, 0, 3, 2, 4).reshape(B, S, MOBA_WIDTH)


def setup_inputs(seed: int = 0) -> dict:
    key = jax.random.key(seed)
    ks = jax.random.split(key, 12)
    L, D = DEPTH, D_MODEL
    x = jax.random.normal(ks[0], (BATCH, SEQ, D), jnp.float32)
    w_in = jax.random.normal(ks[1], (L, D, 3 * MIX_WIDTH), jnp.float32) * D ** -0.5
    col = jnp.arange(3 * MIX_WIDTH)
    is_v = ((col >= 2 * DIFF_WIDTH) & (col < 3 * DIFF_WIDTH)) | (col >= 3 * DIFF_WIDTH + 2 * MOBA_WIDTH)
    w_in = w_in * jnp.where(is_v, BETA, 1.0).astype(jnp.float32)
    diff_lambda = jax.random.normal(ks[2], (L, 4, HEAD_DIM), jnp.float32) * 0.1
    diff_subln_g = 1.0 + 0.02 * jax.random.normal(ks[3], (L, 2 * HEAD_DIM), jnp.float32)
    w_o = jax.random.normal(ks[4], (L, MIX_WIDTH, D), jnp.float32) * MIX_WIDTH ** -0.5 * BETA
    ln1_g = 1.0 + 0.02 * jax.random.normal(ks[5], (L, D), jnp.float32)
    ln1_b = 0.02 * jax.random.normal(ks[6], (L, D), jnp.float32)
    w_ffn_in = jax.random.normal(ks[7], (L, D, 2 * D_FF), jnp.float32) * D ** -0.5 * BETA
    w_ffn_out = jax.random.normal(ks[8], (L, D_FF, D), jnp.float32) * D_FF ** -0.5 * BETA
    ln2_g = 1.0 + 0.02 * jax.random.normal(ks[9], (L, D), jnp.float32)
    ln2_b = 0.02 * jax.random.normal(ks[10], (L, D), jnp.float32)
    return {"x": x, "w_in": w_in, "diff_lambda": diff_lambda, "diff_subln_g": diff_subln_g,
            "w_o": w_o, "ln1_g": ln1_g, "ln1_b": ln1_b, "w_ffn_in": w_ffn_in,
            "w_ffn_out": w_ffn_out, "ln2_g": ln2_g, "ln2_b": ln2_b}


def reference(x, w_in, diff_lambda, diff_subln_g, w_o, ln1_g, ln1_b, w_ffn_in, w_ffn_out, ln2_g, ln2_b):
    S = x.shape[1]
    cos, sin = rope_tables(S)
    splits = [DIFF_WIDTH, 2 * DIFF_WIDTH, 3 * DIFF_WIDTH,
              3 * DIFF_WIDTH + MOBA_WIDTH, 3 * DIFF_WIDTH + 2 * MOBA_WIDTH]
    for l in range(DEPTH):
        lambda_init = lambda_init_for(l)
        proj = jnp.einsum('bsd,de->bse', x, w_in[l])
        dq, dk, dv, mq, mk, mv = jnp.split(proj, splits, axis=-1)
        a_out = diff_attention(dq, dk, dv, diff_lambda[l], diff_subln_g[l], lambda_init, cos, sin)
        b_out = moba_attention(mq, mk, mv, cos, sin)
        mix = jnp.einsum('bse,ed->bsd', jnp.concatenate([a_out, b_out], axis=-1), w_o[l])
        x = layer_norm(ALPHA * x + mix, ln1_g[l], ln1_b[l])
        gu = jnp.einsum('bsd,df->bsf', x, w_ffn_in[l])
        g, u = jnp.split(gu, 2, axis=-1)
        f = jnp.einsum('bsf,fd->bsd', jax.nn.silu(g) * u, w_ffn_out[l])
        x = layer_norm(ALPHA * x + f, ln2_g[l], ln2_b[l])
    return x
```

```python
import functools
import math

import numpy as np
import jax
import jax.numpy as jnp
from jax import lax
from jax.experimental import pallas as pl
from jax.experimental.pallas import tpu as pltpu

D_MODEL = 1024
HEAD_DIM = 64
N_DIFF_HEADS = 4
DIFF_WIDTH = N_DIFF_HEADS * 2 * HEAD_DIM
N_MOBA_HEADS = 8
MOBA_WIDTH = N_MOBA_HEADS * HEAD_DIM
ROT_DIM = HEAD_DIM // 4
ROPE_THETA = 500000.0
MOBA_BLOCK = 256
MOBA_TOPK = 3
D_FF = 2816
DEPTH = 1
ALPHA = (2.0 * DEPTH) ** 0.25
LN_EPS = 1e-5
LAMBDA_INIT = 0.8 - 0.6 * math.exp(-0.3 * 0)
QK_SCALE = HEAD_DIM ** -0.5

V7X_LANES = 128
V7X_VMEM_BYTES = 64 * 1024 * 1024
VMEM_LIMIT = V7X_VMEM_BYTES - 8 * 1024 * 1024

PROJ_ROWS = 512
ATTN_TILE = 256
MIX_ROWS = 512
FFN_ROWS = 256
FFN_CHUNK = 256
GATE_LANES = 128
NEG_BIG = -1e30

_NT = (((1,), (1,)), ((), ()))


def _rope_tables(seq):
    half = ROT_DIM // 2
    inv = ROPE_THETA ** (-np.arange(0, ROT_DIM, 2, dtype=np.float64) / ROT_DIM)
    ang = np.arange(seq, dtype=np.float64)[:, None] * inv[None, :]
    cos = np.ones((seq, V7X_LANES), np.float64)
    sin = np.zeros((seq, V7X_LANES), np.float64)
    hot = np.zeros((seq, V7X_LANES), np.float64)
    for base in (0, HEAD_DIM):
        cos[:, base:base + half] = np.cos(ang)
        cos[:, base + half:base + ROT_DIM] = np.cos(ang)
        sin[:, base:base + half] = -np.sin(ang)
        sin[:, base + half:base + ROT_DIM] = np.sin(ang)
    blk = np.arange(seq) // MOBA_BLOCK
    hot[np.arange(seq), HEAD_DIM + blk] = 1.0
    return (jnp.asarray(cos, jnp.float32), jnp.asarray(sin, jnp.float32),
            jnp.asarray(hot, jnp.float32))


def _proj_kernel(x_ref, w_ref, cos_ref, sin_ref, hot_ref,
                 dq_ref, dk_ref, dv_ref, mq_ref, mk_ref, mv_ref, kmean_ref,
                 *, tiles_per_seq):
    tm = x_ref.shape[0]
    t = pl.program_id(0) % tiles_per_seq
    xb = x_ref[...].astype(jnp.bfloat16)
    cos = cos_ref[...]
    sin = sin_ref[...]
    lane = lax.broadcasted_iota(jnp.int32, (tm, V7X_LANES), 1)
    first_half = (lane & (HEAD_DIM - 1)) < (ROT_DIM // 2)

    def seg(j):
        return jnp.dot(xb, w_ref[:, j * 512:(j + 1) * 512], preferred_element_type=jnp.float32)

    def rope_blocks(a, scale):
        out = []
        for c in range(4):
            blk = a[:, c * V7X_LANES:(c + 1) * V7X_LANES]
            partner = jnp.where(first_half,
                                pltpu.roll(blk, V7X_LANES - ROT_DIM // 2, 1),
                                pltpu.roll(blk, ROT_DIM // 2, 1))
            r = blk * cos + partner * sin
            out.append(r * scale if scale != 1.0 else r)
        return out

    for c, blk in enumerate(rope_blocks(seg(0), QK_SCALE)):
        dq_ref[:, c * V7X_LANES:(c + 1) * V7X_LANES] = blk.astype(jnp.bfloat16)
    for c, blk in enumerate(rope_blocks(seg(1), 1.0)):
        dk_ref[:, c * V7X_LANES:(c + 1) * V7X_LANES] = blk.astype(jnp.bfloat16)
    dv_ref[...] = seg(2).astype(jnp.bfloat16)
    mv_ref[...] = seg(5).astype(jnp.bfloat16)

    qblocks = rope_blocks(seg(3), QK_SCALE)
    kblocks = rope_blocks(seg(4), 1.0)

    @pl.when(t == 0)
    def _():
        kmean_ref[...] = jnp.zeros_like(kmean_ref)

    blocks_per_tile = tm // MOBA_BLOCK
    n_blocks = kmean_ref.shape[0]
    km_row = lax.broadcasted_iota(jnp.int32, (n_blocks, V7X_LANES), 0)
    for c, blk in enumerate(kblocks):
        km_c = kmean_ref[:, c * V7X_LANES:(c + 1) * V7X_LANES]
        for r in range(blocks_per_tile):
            mean = jnp.sum(blk[r * MOBA_BLOCK:(r + 1) * MOBA_BLOCK], axis=0, keepdims=True) \
                * (1.0 / MOBA_BLOCK)
            km_c = jnp.where(km_row == blocks_per_tile * t + r, mean, km_c)
        kmean_ref[:, c * V7X_LANES:(c + 1) * V7X_LANES] = km_c

    km = kmean_ref[...]
    km_rows = jnp.concatenate([km] * N_MOBA_HEADS, axis=0)
    r_id = lax.broadcasted_iota(jnp.int32, km_rows.shape, 0)
    l_id = lax.broadcasted_iota(jnp.int32, km_rows.shape, 1)
    km_rows = jnp.where((r_id // n_blocks) == (l_id // HEAD_DIM), km_rows, 0.0)
    q_all = jnp.concatenate(qblocks, axis=1)
    gate = lax.dot_general(q_all, km_rows, _NT, precision=lax.Precision.HIGHEST,
                           preferred_element_type=jnp.float32)

    row = lax.broadcasted_iota(jnp.int32, (tm, GATE_LANES), 0)
    n_id = lane & (n_blocks - 1)
    own = blocks_per_tile * t + row // MOBA_BLOCK
    past = n_id < own
    gm = jnp.where(past, gate, -jnp.inf)
    rank = jnp.zeros((tm, GATE_LANES), jnp.int32)
    for s in range(1, n_blocks):
        lower = pltpu.roll(gm, s, 1)
        upper = pltpu.roll(gm, GATE_LANES - s, 1)
        rank = rank + jnp.where((n_id >= s) & (lower >= gm), 1, 0)
        rank = rank + jnp.where((n_id < n_blocks - s) & (upper > gm), 1, 0)
    keep = (past & (rank < MOBA_TOPK)) | (n_id == own)
    bias = jnp.where(keep, 0.0, NEG_BIG)

    hot = hot_ref[...]
    for h in range(N_MOBA_HEADS):
        c = h // 2
        qh = qblocks[c]
        kh = kblocks[c]
        if h % 2:
            qh = pltpu.roll(qh, HEAD_DIM, 1)
            kh = pltpu.roll(kh, HEAD_DIM, 1)
        shift = (HEAD_DIM - n_blocks * h) % GATE_LANES
        bh = pltpu.roll(bias, shift, 1) if shift else bias
        q_aug = jnp.where(lane < HEAD_DIM, qh, jnp.where(lane < HEAD_DIM + n_blocks, bh, 0.0))
        k_aug = jnp.where(lane < HEAD_DIM, kh, hot)
        mq_ref[:, h * V7X_LANES:(h + 1) * V7X_LANES] = q_aug.astype(jnp.bfloat16)
        mk_ref[:, h * V7X_LANES:(h + 1) * V7X_LANES] = k_aug.astype(jnp.bfloat16)


def _proj(x2d, w_bf16, seq):
    m = x2d.shape[0]
    tm = PROJ_ROWS
    tiles_per_seq = seq // tm
    n_blocks = seq // MOBA_BLOCK
    assert n_blocks * N_MOBA_HEADS == GATE_LANES
    cos, sin, hot = _rope_tables(seq)
    row_tile = lambda w: pl.BlockSpec((tm, w), lambda i: (i, 0))
    table = pl.BlockSpec((tm, V7X_LANES), lambda i: (i % tiles_per_seq, 0))
    bf = lambda w: jax.ShapeDtypeStruct((m, w), jnp.bfloat16)
    outs = pl.pallas_call(
        functools.partial(_proj_kernel, tiles_per_seq=tiles_per_seq),
        grid=(m // tm,),
        in_specs=[row_tile(D_MODEL),
                  pl.BlockSpec(w_bf16.shape, lambda i: (0, 0)),
                  table, table, table],
        out_specs=[row_tile(DIFF_WIDTH), row_tile(DIFF_WIDTH), row_tile(DIFF_WIDTH),
                   row_tile(2 * MOBA_WIDTH), row_tile(2 * MOBA_WIDTH), row_tile(MOBA_WIDTH)],
        out_shape=[bf(DIFF_WIDTH), bf(DIFF_WIDTH), bf(DIFF_WIDTH),
                   bf(2 * MOBA_WIDTH), bf(2 * MOBA_WIDTH), bf(MOBA_WIDTH)],
        scratch_shapes=[pltpu.VMEM((n_blocks, MOBA_WIDTH), jnp.float32)],
        compiler_params=pltpu.CompilerParams(dimension_semantics=("arbitrary",),
                                             vmem_limit_bytes=VMEM_LIMIT),
        name="proj",
    )(x2d, w_bf16, cos, sin, hot)
    return outs


def _scores(q, k):
    return lax.dot_general(q, k, _NT, preferred_element_type=jnp.float32)


def _first_tile(s, v):
    m = jnp.max(s, axis=1, keepdims=True)
    p = jnp.exp(s - m)
    l = jnp.sum(p, axis=1, keepdims=True)
    return m, l, p.astype(jnp.bfloat16)


def _next_tile(s, m, l):
    m_new = jnp.maximum(m, jnp.max(s, axis=1, keepdims=True))
    alpha = jnp.exp(m - m_new)
    p = jnp.exp(s - m_new)
    l_new = alpha * l + jnp.sum(p, axis=1, keepdims=True)
    return m_new, l_new, alpha, p.astype(jnp.bfloat16)


def _causal_mask(t):
    row = lax.broadcasted_iota(jnp.int32, (t, t), 0)
    col = lax.broadcasted_iota(jnp.int32, (t, t), 1)
    return col <= row


def _diff_kernel(lam_ref, g_ref, q_ref, k_ref, v_ref, o_ref):
    t = q_ref.shape[0]
    i = pl.program_id(2)
    q = q_ref[...]
    lane = lax.broadcasted_iota(jnp.int32, q.shape, 1)
    zero = jnp.zeros_like(q)
    q1 = jnp.where(lane < HEAD_DIM, q, zero)
    q2 = jnp.where(lane >= HEAD_DIM, q, zero)

    def kv(j):
        start = pl.multiple_of(j * t, t)
        return k_ref[pl.ds(start, t), :], v_ref[pl.ds(start, t), :]

    kd, vd = kv(i)
    mask = _causal_mask(t)
    m1, l1, p1 = _first_tile(jnp.where(mask, _scores(q1, kd), -jnp.inf), vd)
    m2, l2, p2 = _first_tile(jnp.where(mask, _scores(q2, kd), -jnp.inf), vd)
    acc1 = jnp.dot(p1, vd, preferred_element_type=jnp.float32)
    acc2 = jnp.dot(p2, vd, preferred_element_type=jnp.float32)

    def body(j, carry):
        m1, l1, acc1, m2, l2, acc2 = carry
        kj, vj = kv(j)
        m1, l1, a1, p1 = _next_tile(_scores(q1, kj), m1, l1)
        m2, l2, a2, p2 = _next_tile(_scores(q2, kj), m2, l2)
        acc1 = a1 * acc1 + jnp.dot(p1, vj, preferred_element_type=jnp.float32)
        acc2 = a2 * acc2 + jnp.dot(p2, vj, preferred_element_type=jnp.float32)
        return m1, l1, acc1, m2, l2, acc2

    m1, l1, acc1, m2, l2, acc2 = lax.fori_loop(0, i, body, (m1, l1, acc1, m2, l2, acc2))

    lp = lam_ref[...]
    lam = (jnp.exp(jnp.sum(lp[0:1] * lp[1:2], axis=1, keepdims=True))
           - jnp.exp(jnp.sum(lp[2:3] * lp[3:4], axis=1, keepdims=True)) + LAMBDA_INIT)
    o = acc1 / l1 - lam * (acc2 / l2)
    ms = jnp.mean(o * o, axis=1, keepdims=True)
    o = o * lax.rsqrt(ms + LN_EPS)
    o = o * g_ref[...] * (1.0 - LAMBDA_INIT)
    o_ref[...] = o.astype(o_ref.dtype)


def _diff_attn(dq, dk, dv, lam_params, subln_g, batch, seq):
    t = ATTN_TILE
    nq = seq // t
    width = 2 * HEAD_DIM
    q_spec = pl.BlockSpec((t, width), lambda b, h, i: (b * nq + i, h))
    kv_spec = pl.BlockSpec((seq, width), lambda b, h, i: (b, h))
    return pl.pallas_call(
        _diff_kernel,
        grid=(batch, N_DIFF_HEADS, nq),
        in_specs=[pl.BlockSpec(lam_params.shape, lambda b, h, i: (0, 0)),
                  pl.BlockSpec(subln_g.shape, lambda b, h, i: (0, 0)),
                  q_spec, kv_spec, kv_spec],
        out_specs=q_spec,
        out_shape=jax.ShapeDtypeStruct(dq.shape, jnp.bfloat16),
        compiler_params=pltpu.CompilerParams(
            dimension_semantics=("arbitrary", "arbitrary", "arbitrary"),
            vmem_limit_bytes=VMEM_LIMIT),
        name="diff_attn",
    )(lam_params, subln_g, dq, dk, dv)


def _moba_kernel(q_ref, k_ref, v_ref, o_ref):
    t = q_ref.shape[0]
    i = pl.program_id(2)
    q = q_ref[...]
    qa = q[:, :V7X_LANES]
    qb = q[:, V7X_LANES:]
    low = lax.broadcasted_iota(jnp.int32, (t, V7X_LANES), 1) < HEAD_DIM

    def kv(j):
        start = pl.multiple_of(j * t, t)
        return k_ref[pl.ds(start, t), :], v_ref[pl.ds(start, t), :]

    kd, vd = kv(i)
    mask = _causal_mask(t)
    ma, la, pa = _first_tile(jnp.where(mask, _scores(qa, kd[:, :V7X_LANES]), -jnp.inf), vd)
    mb, lb, pb = _first_tile(jnp.where(mask, _scores(qb, kd[:, V7X_LANES:]), -jnp.inf), vd)
    acc = jnp.where(low, jnp.dot(pa, vd, preferred_element_type=jnp.float32),
                    jnp.dot(pb, vd, preferred_element_type=jnp.float32))

    def body(j, carry):
        ma, la, mb, lb, acc = carry
        kj, vj = kv(j)
        ma, la, aa, pa = _next_tile(_scores(qa, kj[:, :V7X_LANES]), ma, la)
        mb, lb, ab, pb = _next_tile(_scores(qb, kj[:, V7X_LANES:]), mb, lb)
        pv = jnp.where(low, jnp.dot(pa, vj, preferred_element_type=jnp.float32),
                       jnp.dot(pb, vj, preferred_element_type=jnp.float32))
        acc = jnp.where(low, aa, ab) * acc + pv
        return ma, la, mb, lb, acc

    ma, la, mb, lb, acc = lax.fori_loop(0, i, body, (ma, la, mb, lb, acc))
    o_ref[...] = (acc / jnp.where(low, la, lb)).astype(o_ref.dtype)


def _moba_attn(mq, mk, mv, batch, seq):
    t = ATTN_TILE
    nq = seq // t
    pairs = N_MOBA_HEADS // 2
    return pl.pallas_call(
        _moba_kernel,
        grid=(batch, pairs, nq),
        in_specs=[pl.BlockSpec((t, 2 * V7X_LANES), lambda b, g, i: (b * nq + i, g)),
                  pl.BlockSpec((seq, 2 * V7X_LANES), lambda b, g, i: (b, g)),
                  pl.BlockSpec((seq, V7X_LANES), lambda b, g, i: (b, g))],
        out_specs=pl.BlockSpec((t, V7X_LANES), lambda b, g, i: (b * nq + i, g)),
        out_shape=jax.ShapeDtypeStruct(mv.shape, jnp.bfloat16),
        compiler_params=pltpu.CompilerParams(
            dimension_semantics=("arbitrary", "arbitrary", "arbitrary"),
            vmem_limit_bytes=VMEM_LIMIT),
        name="moba_attn",
    )(mq, mk, mv)


def _layer_norm(y, g, b):
    mu = jnp.mean(y, axis=-1, keepdims=True)
    d = y - mu
    var = jnp.mean(d * d, axis=-1, keepdims=True)
    return d * lax.rsqrt(var + LN_EPS) * g + b


def _mix_ln_kernel(a_ref, b_ref, x_ref, wo_ref, g_ref, beta_ref, o_ref):
    mix = jnp.dot(a_ref[...], wo_ref[:DIFF_WIDTH, :], preferred_element_type=jnp.float32)
    mix = mix + jnp.dot(b_ref[...], wo_ref[DIFF_WIDTH:, :], preferred_element_type=jnp.float32)
    y = ALPHA * x_ref[...] + mix
    o_ref[...] = _layer_norm(y, g_ref[...], beta_ref[...])


def _mix_ln(a_out, b_out, x2d, wo_bf16, g, beta):
    m = x2d.shape[0]
    tm = MIX_ROWS
    row_tile = lambda w: pl.BlockSpec((tm, w), lambda i: (i, 0))
    whole = lambda a: pl.BlockSpec(a.shape, lambda i: (0, 0))
    return pl.pallas_call(
        _mix_ln_kernel,
        grid=(m // tm,),
        in_specs=[row_tile(DIFF_WIDTH), row_tile(MOBA_WIDTH), row_tile(D_MODEL),
                  whole(wo_bf16), whole(g), whole(beta)],
        out_specs=row_tile(D_MODEL),
        out_shape=jax.ShapeDtypeStruct(x2d.shape, jnp.float32),
        compiler_params=pltpu.CompilerParams(dimension_semantics=("arbitrary",),
                                             vmem_limit_bytes=VMEM_LIMIT),
        name="mix_ln",
    )(a_out, b_out, x2d, wo_bf16, g, beta)


def _ffn_ln_kernel(x_ref, win_ref, wout_ref, g_ref, beta_ref, o_ref, acc_ref):
    x = x_ref[...]
    xb = x.astype(jnp.bfloat16)
    for c in range(D_FF // FFN_CHUNK):
        lo = c * FFN_CHUNK
        gate = jnp.dot(xb, win_ref[:, lo:lo + FFN_CHUNK], preferred_element_type=jnp.float32)
        up = jnp.dot(xb, win_ref[:, D_FF + lo:D_FF + lo + FFN_CHUNK],
                     preferred_element_type=jnp.float32)
        h = (gate * jax.nn.sigmoid(gate) * up).astype(jnp.bfloat16)
        part = jnp.dot(h, wout_ref[lo:lo + FFN_CHUNK, :], preferred_element_type=jnp.float32)
        if c == 0:
            acc_ref[...] = part
        else:
            acc_ref[...] += part
    y = ALPHA * x + acc_ref[...]
    o_ref[...] = _layer_norm(y, g_ref[...], beta_ref[...])


def _ffn_ln(x1, win_bf16, wout_bf16, g, beta):
    m = x1.shape[0]
    tm = FFN_ROWS
    assert D_FF % FFN_CHUNK == 0
    row_tile = pl.BlockSpec((tm, D_MODEL), lambda i: (i, 0))
    whole = lambda a: pl.BlockSpec(a.shape, lambda i: (0, 0))
    return pl.pallas_call(
        _ffn_ln_kernel,
        grid=(m // tm,),
        in_specs=[row_tile, whole(win_bf16), whole(wout_bf16), whole(g), whole(beta)],
        out_specs=row_tile,
        out_shape=jax.ShapeDtypeStruct(x1.shape, jnp.float32),
        scratch_shapes=[pltpu.VMEM((tm, D_MODEL), jnp.float32)],
        compiler_params=pltpu.CompilerParams(dimension_semantics=("arbitrary",),
                                             vmem_limit_bytes=VMEM_LIMIT),
        name="ffn_ln",
    )(x1, win_bf16, wout_bf16, g, beta)


def kernel(x, w_in, diff_lambda, diff_subln_g, w_o, ln1_g, ln1_b, w_ffn_in, w_ffn_out, ln2_g, ln2_b):
    batch, seq, d = x.shape
    assert d == D_MODEL and seq % PROJ_ROWS == 0 and w_in.shape[0] == DEPTH
    x2d = x.reshape(batch * seq, d)
    for l in range(DEPTH):
        dq, dk, dv, mq, mk, mv = _proj(x2d, w_in[l].astype(jnp.bfloat16), seq)
        a_out = _diff_attn(dq, dk, dv, diff_lambda[l], diff_subln_g[l][None, :], batch, seq)
        b_out = _moba_attn(mq, mk, mv, batch, seq)
        x1 = _mix_ln(a_out, b_out, x2d, w_o[l].astype(jnp.bfloat16),
                     ln1_g[l][None, :], ln1_b[l][None, :])
        x2d = _ffn_ln(x1, w_ffn_in[l].astype(jnp.bfloat16), w_ffn_out[l].astype(jnp.bfloat16),
                      ln2_g[l][None, :], ln2_b[l][None, :])
    return x2d.reshape(batch, seq, d)
```

```python
import functools
import math

import numpy as np
import jax
import jax.numpy as jnp
from jax import lax
from jax.experimental import pallas as pl
from jax.experimental.pallas import tpu as pltpu

D_MODEL = 1024
HEAD_DIM = 64
N_DIFF_HEADS = 4
DIFF_WIDTH = N_DIFF_HEADS * 2 * HEAD_DIM
N_MOBA_HEADS = 8
MOBA_WIDTH = N_MOBA_HEADS * HEAD_DIM
ROT_DIM = HEAD_DIM // 4
ROPE_THETA = 500000.0
MOBA_BLOCK = 256
MOBA_TOPK = 3
D_FF = 2816
DEPTH = 1
ALPHA = (2.0 * DEPTH) ** 0.25
LN_EPS = 1e-5
LAMBDA_INIT = 0.8 - 0.6 * math.exp(-0.3 * 0)
QK_SCALE = HEAD_DIM ** -0.5

V7X_LANES = 128
V7X_VMEM_BYTES = 64 * 1024 * 1024
VMEM_LIMIT = V7X_VMEM_BYTES - 8 * 1024 * 1024

ATTN_TILE = 512
PROJ_ROWS = ATTN_TILE
MIX_ROWS = 512
FFN_ROWS = 256
FFN_CHUNK = 256
GATE_LANES = 128
NEG_BIG = -1e30

_NT = (((1,), (1,)), ((), ()))


def _rope_tables(seq):
    half = ROT_DIM // 2
    inv = ROPE_THETA ** (-np.arange(0, ROT_DIM, 2, dtype=np.float64) / ROT_DIM)
    ang = np.arange(seq, dtype=np.float64)[:, None] * inv[None, :]
    cos = np.ones((seq, V7X_LANES), np.float64)
    sin = np.zeros((seq, V7X_LANES), np.float64)
    hot = np.zeros((seq, V7X_LANES), np.float64)
    for base in (0, HEAD_DIM):
        cos[:, base:base + half] = np.cos(ang)
        cos[:, base + half:base + ROT_DIM] = np.cos(ang)
        sin[:, base:base + half] = -np.sin(ang)
        sin[:, base + half:base + ROT_DIM] = np.sin(ang)
    blk = np.arange(seq) // MOBA_BLOCK
    hot[np.arange(seq), HEAD_DIM + blk] = 1.0
    return (jnp.asarray(cos, jnp.float32), jnp.asarray(sin, jnp.float32),
            jnp.asarray(hot, jnp.float32))


def _proj_kernel(x_ref, w_ref, wvt_ref, cos_ref, sin_ref, hot_ref,
                 dq_ref, dk_ref, mq_ref, mk_ref, dvt_ref, mvt_ref, kmean_ref,
                 *, tiles_per_seq):
    tm = x_ref.shape[0]
    t = pl.program_id(0) % tiles_per_seq
    xb = x_ref[...].astype(jnp.bfloat16)
    cos = cos_ref[...]
    sin = sin_ref[...]
    lane = lax.broadcasted_iota(jnp.int32, (tm, V7X_LANES), 1)
    first_half = (lane & (HEAD_DIM - 1)) < (ROT_DIM // 2)

    def seg(j):
        return jnp.dot(xb, w_ref[:, j * 512:(j + 1) * 512], preferred_element_type=jnp.float32)

    def rope_blocks(a, scale):
        out = []
        for c in range(4):
            blk = a[:, c * V7X_LANES:(c + 1) * V7X_LANES]
            partner = jnp.where(first_half,
                                pltpu.roll(blk, V7X_LANES - ROT_DIM // 2, 1),
                                pltpu.roll(blk, ROT_DIM // 2, 1))
            r = blk * cos + partner * sin
            out.append(r * scale if scale != 1.0 else r)
        return out

    for c, blk in enumerate(rope_blocks(seg(0), QK_SCALE)):
        dq_ref[:, c * V7X_LANES:(c + 1) * V7X_LANES] = blk.astype(jnp.bfloat16)
    for c, blk in enumerate(rope_blocks(seg(1), 1.0)):
        dk_ref[:, c * V7X_LANES:(c + 1) * V7X_LANES] = blk.astype(jnp.bfloat16)

    vt = lax.dot_general(wvt_ref[...], xb, _NT, preferred_element_type=jnp.float32)
    for h in range(dvt_ref.shape[1]):
        dvt_ref[0, h] = vt[h * V7X_LANES:(h + 1) * V7X_LANES].astype(jnp.bfloat16)
    for g in range(mvt_ref.shape[1]):
        lo = DIFF_WIDTH + g * V7X_LANES
        mvt_ref[0, g] = vt[lo:lo + V7X_LANES].astype(jnp.bfloat16)

    qblocks = rope_blocks(seg(2), QK_SCALE)
    kblocks = rope_blocks(seg(3), 1.0)

    @pl.when(t == 0)
    def _():
        kmean_ref[...] = jnp.zeros_like(kmean_ref)

    blocks_per_tile = tm // MOBA_BLOCK
    n_blocks = kmean_ref.shape[0]
    km_row = lax.broadcasted_iota(jnp.int32, (n_blocks, V7X_LANES), 0)
    for c, blk in enumerate(kblocks):
        km_c = kmean_ref[:, c * V7X_LANES:(c + 1) * V7X_LANES]
        for r in range(blocks_per_tile):
            mean = jnp.sum(blk[r * MOBA_BLOCK:(r + 1) * MOBA_BLOCK], axis=0, keepdims=True) \
                * (1.0 / MOBA_BLOCK)
            km_c = jnp.where(km_row == blocks_per_tile * t + r, mean, km_c)
        kmean_ref[:, c * V7X_LANES:(c + 1) * V7X_LANES] = km_c

    km = kmean_ref[...]
    km_rows = jnp.concatenate([km] * N_MOBA_HEADS, axis=0)
    r_id = lax.broadcasted_iota(jnp.int32, km_rows.shape, 0)
    l_id = lax.broadcasted_iota(jnp.int32, km_rows.shape, 1)
    km_rows = jnp.where((r_id // n_blocks) == (l_id // HEAD_DIM), km_rows, 0.0)
    q_all = jnp.concatenate(qblocks, axis=1)
    gate = lax.dot_general(q_all, km_rows, _NT, precision=lax.Precision.HIGHEST,
                           preferred_element_type=jnp.float32)

    row = lax.broadcasted_iota(jnp.int32, (tm, GATE_LANES), 0)
    n_id = lane & (n_blocks - 1)
    own = blocks_per_tile * t + row // MOBA_BLOCK
    past = n_id < own
    gm = jnp.where(past, gate, -jnp.inf)
    rank = jnp.zeros((tm, GATE_LANES), jnp.int32)
    for s in range(1, n_blocks):
        lower = pltpu.roll(gm, s, 1)
        upper = pltpu.roll(gm, GATE_LANES - s, 1)
        rank = rank + jnp.where((n_id >= s) & (lower >= gm), 1, 0)
        rank = rank + jnp.where((n_id < n_blocks - s) & (upper > gm), 1, 0)
    keep = (past & (rank < MOBA_TOPK)) | (n_id == own)
    bias = jnp.where(keep, 0.0, NEG_BIG)

    hot = hot_ref[...]
    for h in range(N_MOBA_HEADS):
        c = h // 2
        qh = qblocks[c]
        kh = kblocks[c]
        if h % 2:
            qh = pltpu.roll(qh, HEAD_DIM, 1)
            kh = pltpu.roll(kh, HEAD_DIM, 1)
        shift = (HEAD_DIM - n_blocks * h) % GATE_LANES
        bh = pltpu.roll(bias, shift, 1) if shift else bias
        q_aug = jnp.where(lane < HEAD_DIM, qh, jnp.where(lane < HEAD_DIM + n_blocks, bh, 0.0))
        k_aug = jnp.where(lane < HEAD_DIM, kh, hot)
        mq_ref[:, h * V7X_LANES:(h + 1) * V7X_LANES] = q_aug.astype(jnp.bfloat16)
        mk_ref[:, h * V7X_LANES:(h + 1) * V7X_LANES] = k_aug.astype(jnp.bfloat16)


def _proj(x2d, w_qk, w_vt, seq):
    m = x2d.shape[0]
    tm = PROJ_ROWS
    tiles_per_seq = seq // tm
    n_blocks = seq // MOBA_BLOCK
    assert n_blocks * N_MOBA_HEADS == GATE_LANES
    cos, sin, hot = _rope_tables(seq)
    row_tile = lambda w: pl.BlockSpec((tm, w), lambda i: (i, 0))
    whole = lambda a: pl.BlockSpec(a.shape, lambda i: (0, 0))
    table = pl.BlockSpec((tm, V7X_LANES), lambda i: (i % tiles_per_seq, 0))
    vt_tile = lambda n: pl.BlockSpec((1, n, V7X_LANES, tm), lambda i: (i, 0, 0, 0))
    bf = lambda w: jax.ShapeDtypeStruct((m, w), jnp.bfloat16)
    vt_shape = lambda n: jax.ShapeDtypeStruct((m // tm, n, V7X_LANES, tm), jnp.bfloat16)
    n_dv = DIFF_WIDTH // V7X_LANES
    n_mv = MOBA_WIDTH // V7X_LANES
    return pl.pallas_call(
        functools.partial(_proj_kernel, tiles_per_seq=tiles_per_seq),
        grid=(m // tm,),
        in_specs=[row_tile(D_MODEL), whole(w_qk), whole(w_vt), table, table, table],
        out_specs=[row_tile(DIFF_WIDTH), row_tile(DIFF_WIDTH),
                   row_tile(2 * MOBA_WIDTH), row_tile(2 * MOBA_WIDTH),
                   vt_tile(n_dv), vt_tile(n_mv)],
        out_shape=[bf(DIFF_WIDTH), bf(DIFF_WIDTH), bf(2 * MOBA_WIDTH), bf(2 * MOBA_WIDTH),
                   vt_shape(n_dv), vt_shape(n_mv)],
        scratch_shapes=[pltpu.VMEM((n_blocks, MOBA_WIDTH), jnp.float32)],
        compiler_params=pltpu.CompilerParams(dimension_semantics=("arbitrary",),
                                             vmem_limit_bytes=VMEM_LIMIT),
        name="proj",
    )(x2d, w_qk, w_vt, cos, sin, hot)


def _scores_t(k, q):
    return lax.dot_general(k, q, _NT, preferred_element_type=jnp.float32)


def _causal_mask_t(t):
    key = lax.broadcasted_iota(jnp.int32, (t, t), 0)
    qry = lax.broadcasted_iota(jnp.int32, (t, t), 1)
    return key <= qry


def _flash_two_streams(i, t, scores, pv, s_ref, p_ref, acc_ref):
    def nth(n):
        return jnp.where(n <= 0, i, jnp.minimum(n - 1, i))

    def put_scores(n, slot, mask=None):
        s = scores(nth(n))
        for x in range(2):
            s_ref[slot, x] = s[x] if mask is None else jnp.where(mask, s[x], -jnp.inf)

    def softmax(n, slot, m, l):
        live = n <= i
        m_out, l_out, alpha = [], [], []
        for x in range(2):
            tile_max = jnp.max(s_ref[slot, x], axis=0, keepdims=True)
            m_new = jnp.maximum(m[x], jnp.where(live, tile_max, -jnp.inf))
            a = jnp.exp(m[x] - m_new)
            p = jnp.exp(s_ref[slot, x] - jnp.where(live, m_new, jnp.inf))
            p_ref[slot, x] = p.astype(jnp.bfloat16)
            m_out.append(m_new)
            l_out.append(a * l[x] + jnp.sum(p, axis=0, keepdims=True))
            alpha.append(a)
        return m_out, l_out, alpha

    def add_values(n, slot, alpha):
        o = pv(nth(n), [p_ref[slot, 0], p_ref[slot, 1]])
        for x in range(2):
            acc_ref[x] = alpha[x] * acc_ref[x] + o[x]

    row = lambda v: jnp.full((1, t), v, jnp.float32)
    put_scores(0, 0, _causal_mask_t(t))
    p_ref[1] = jnp.zeros(p_ref.shape[1:], p_ref.dtype)
    acc_ref[...] = jnp.zeros(acc_ref.shape, acc_ref.dtype)

    def two_steps(u, carry):
        m, l, a_prev = carry
        n = 2 * u
        put_scores(n + 1, 1)
        m, l, a_cur = softmax(n, 0, m, l)
        add_values(n - 1, 1, a_prev)
        put_scores(n + 2, 0)
        m, l, a_next = softmax(n + 1, 1, m, l)
        add_values(n, 0, a_cur)
        return m, l, a_next

    trips = (i + 1) // 2
    m, l, a_prev = lax.fori_loop(0, trips, two_steps,
                                 ([row(-jnp.inf)] * 2, [row(0.0)] * 2, [row(1.0)] * 2))
    n = 2 * trips
    m, l, a_last = softmax(n, 0, m, l)
    add_values(n - 1, 1, a_prev)
    o = pv(nth(n), [p_ref[0, 0], p_ref[0, 1]])
    return [(a_last[x] * acc_ref[x] + o[x], l[x]) for x in range(2)]


def _flash_scratch(t, acc_rows):
    return [pltpu.VMEM((2, 2, t, t), jnp.float32), pltpu.VMEM((2, 2, t, t), jnp.bfloat16),
            pltpu.VMEM((2, acc_rows, t), jnp.float32)]


def _diff_kernel(lam_ref, g_ref, q_ref, k_ref, vt_ref, o_ref, s_ref, p_ref, acc_ref):
    t = q_ref.shape[0]
    i = pl.program_id(2)
    q = q_ref[...]
    lane = lax.broadcasted_iota(jnp.int32, q.shape, 1)
    zero = jnp.zeros_like(q)
    qs = [jnp.where(lane < HEAD_DIM, q, zero), jnp.where(lane >= HEAD_DIM, q, zero)]

    def scores(j):
        k = k_ref[pl.ds(pl.multiple_of(j * t, t), t), :]
        return [_scores_t(k, qs[x]) for x in range(2)]

    def pv(j, p):
        vt = vt_ref[j, 0]
        return [jnp.dot(vt, p[x], preferred_element_type=jnp.float32) for x in range(2)]

    (acc1, l1), (acc2, l2) = _flash_two_streams(i, t, scores, pv, s_ref, p_ref, acc_ref)

    lp = lam_ref[...]
    lam = (jnp.exp(jnp.sum(lp[0:1] * lp[1:2], axis=1, keepdims=True))
           - jnp.exp(jnp.sum(lp[2:3] * lp[3:4], axis=1, keepdims=True)) + LAMBDA_INIT)
    o = acc1 / l1 - lam * (acc2 / l2)
    ms = jnp.mean(o * o, axis=0, keepdims=True)
    o = (o * lax.rsqrt(ms + LN_EPS)).T
    o = o * g_ref[...] * (1.0 - LAMBDA_INIT)
    o_ref[...] = o.astype(o_ref.dtype)


def _diff_attn(dq, dk, dvt, lam_params, subln_g, batch, seq):
    t = ATTN_TILE
    nq = seq // t
    width = 2 * HEAD_DIM
    q_spec = pl.BlockSpec((t, width), lambda b, h, i: (b * nq + i, h))
    k_spec = pl.BlockSpec((seq, width), lambda b, h, i: (b, h))
    vt_spec = pl.BlockSpec((nq, 1, width, t), lambda b, h, i: (b, h, 0, 0))
    return pl.pallas_call(
        _diff_kernel,
        grid=(batch, N_DIFF_HEADS, nq),
        in_specs=[pl.BlockSpec(lam_params.shape, lambda b, h, i: (0, 0)),
                  pl.BlockSpec(subln_g.shape, lambda b, h, i: (0, 0)),
                  q_spec, k_spec, vt_spec],
        out_specs=q_spec,
        out_shape=jax.ShapeDtypeStruct(dq.shape, jnp.bfloat16),
        scratch_shapes=_flash_scratch(t, width),
        compiler_params=pltpu.CompilerParams(
            dimension_semantics=("arbitrary", "arbitrary", "arbitrary"),
            vmem_limit_bytes=VMEM_LIMIT),
        name="diff_attn",
    )(lam_params, subln_g, dq, dk, dvt)


def _moba_kernel(q_ref, k_ref, vt_ref, o_ref, s_ref, p_ref, acc_ref):
    t = q_ref.shape[0]
    i = pl.program_id(2)
    q = q_ref[...]
    qs = [q[:, :V7X_LANES], q[:, V7X_LANES:]]

    def scores(j):
        k = k_ref[pl.ds(pl.multiple_of(j * t, t), t), :]
        return [_scores_t(k[:, x * V7X_LANES:(x + 1) * V7X_LANES], qs[x]) for x in range(2)]

    def pv(j, p):
        vt = vt_ref[j, 0]
        return [jnp.dot(vt[x * HEAD_DIM:(x + 1) * HEAD_DIM], p[x],
                        preferred_element_type=jnp.float32) for x in range(2)]

    (acca, la), (accb, lb) = _flash_two_streams(i, t, scores, pv, s_ref, p_ref, acc_ref)
    o = jnp.concatenate([acca / la, accb / lb], axis=0)
    o_ref[...] = o.T.astype(o_ref.dtype)


def _moba_attn(mq, mk, mvt, batch, seq):
    t = ATTN_TILE
    nq = seq // t
    pairs = N_MOBA_HEADS // 2
    return pl.pallas_call(
        _moba_kernel,
        grid=(batch, pairs, nq),
        in_specs=[pl.BlockSpec((t, 2 * V7X_LANES), lambda b, g, i: (b * nq + i, g)),
                  pl.BlockSpec((seq, 2 * V7X_LANES), lambda b, g, i: (b, g)),
                  pl.BlockSpec((nq, 1, V7X_LANES, t), lambda b, g, i: (b, g, 0, 0))],
        out_specs=pl.BlockSpec((t, V7X_LANES), lambda b, g, i: (b * nq + i, g)),
        out_shape=jax.ShapeDtypeStruct((batch * seq, MOBA_WIDTH), jnp.bfloat16),
        scratch_shapes=_flash_scratch(t, HEAD_DIM),
        compiler_params=pltpu.CompilerParams(
            dimension_semantics=("arbitrary", "arbitrary", "arbitrary"),
            vmem_limit_bytes=VMEM_LIMIT),
        name="moba_attn",
    )(mq, mk, mvt)


def _layer_norm(y, g, b):
    mu = jnp.mean(y, axis=-1, keepdims=True)
    d = y - mu
    var = jnp.mean(d * d, axis=-1, keepdims=True)
    return d * lax.rsqrt(var + LN_EPS) * g + b


def _mix_ln_kernel(a_ref, b_ref, x_ref, wo_ref, g_ref, beta_ref, o_ref):
    mix = jnp.dot(a_ref[...], wo_ref[:DIFF_WIDTH, :], preferred_element_type=jnp.float32)
    mix = mix + jnp.dot(b_ref[...], wo_ref[DIFF_WIDTH:, :], preferred_element_type=jnp.float32)
    y = ALPHA * x_ref[...] + mix
    o_ref[...] = _layer_norm(y, g_ref[...], beta_ref[...])


def _mix_ln(a_out, b_out, x2d, wo_bf16, g, beta):
    m = x2d.shape[0]
    tm = MIX_ROWS
    row_tile = lambda w: pl.BlockSpec((tm, w), lambda i: (i, 0))
    whole = lambda a: pl.BlockSpec(a.shape, lambda i: (0, 0))
    return pl.pallas_call(
        _mix_ln_kernel,
        grid=(m // tm,),
        in_specs=[row_tile(DIFF_WIDTH), row_tile(MOBA_WIDTH), row_tile(D_MODEL),
                  whole(wo_bf16), whole(g), whole(beta)],
        out_specs=row_tile(D_MODEL),
        out_shape=jax.ShapeDtypeStruct(x2d.shape, jnp.float32),
        compiler_params=pltpu.CompilerParams(dimension_semantics=("arbitrary",),
                                             vmem_limit_bytes=VMEM_LIMIT),
        name="mix_ln",
    )(a_out, b_out, x2d, wo_bf16, g, beta)


def _ffn_ln_kernel(x_ref, win_ref, wout_ref, g_ref, beta_ref, o_ref, acc_ref):
    x = x_ref[...]
    xb = x.astype(jnp.bfloat16)
    for c in range(D_FF // FFN_CHUNK):
        lo = c * FFN_CHUNK
        gate = jnp.dot(xb, win_ref[:, lo:lo + FFN_CHUNK], preferred_element_type=jnp.float32)
        up = jnp.dot(xb, win_ref[:, D_FF + lo:D_FF + lo + FFN_CHUNK],
                     preferred_element_type=jnp.float32)
        h = (gate * jax.nn.sigmoid(gate) * up).astype(jnp.bfloat16)
        part = jnp.dot(h, wout_ref[lo:lo + FFN_CHUNK, :], preferred_element_type=jnp.float32)
        if c == 0:
            acc_ref[...] = part
        else:
            acc_ref[...] += part
    y = ALPHA * x + acc_ref[...]
    o_ref[...] = _layer_norm(y, g_ref[...], beta_ref[...])


def _ffn_ln(x1, win_bf16, wout_bf16, g, beta):
    m = x1.shape[0]
    tm = FFN_ROWS
    assert D_FF % FFN_CHUNK == 0
    row_tile = pl.BlockSpec((tm, D_MODEL), lambda i: (i, 0))
    whole = lambda a: pl.BlockSpec(a.shape, lambda i: (0, 0))
    return pl.pallas_call(
        _ffn_ln_kernel,
        grid=(m // tm,),
        in_specs=[row_tile, whole(win_bf16), whole(wout_bf16), whole(g), whole(beta)],
        out_specs=row_tile,
        out_shape=jax.ShapeDtypeStruct(x1.shape, jnp.float32),
        scratch_shapes=[pltpu.VMEM((tm, D_MODEL), jnp.float32)],
        compiler_params=pltpu.CompilerParams(dimension_semantics=("arbitrary",),
                                             vmem_limit_bytes=VMEM_LIMIT),
        name="ffn_ln",
    )(x1, win_bf16, wout_bf16, g, beta)


def _split_w_in(w):
    dv0, mq0, mv0 = 2 * DIFF_WIDTH, 3 * DIFF_WIDTH, 3 * DIFF_WIDTH + 2 * MOBA_WIDTH
    w_qk = jnp.concatenate([w[:, :dv0], w[:, mq0:mv0]], axis=1)
    w_v = jnp.concatenate([w[:, dv0:mq0], w[:, mv0:]], axis=1)
    return w_qk.astype(jnp.bfloat16), w_v.T.astype(jnp.bfloat16)


def kernel(x, w_in, diff_lambda, diff_subln_g, w_o, ln1_g, ln1_b, w_ffn_in, w_ffn_out, ln2_g, ln2_b):
    batch, seq, d = x.shape
    assert d == D_MODEL and seq % ATTN_TILE == 0 and w_in.shape[0] == DEPTH
    x2d = x.reshape(batch * seq, d)
    for l in range(DEPTH):
        w_qk, w_vt = _split_w_in(w_in[l])
        dq, dk, mq, mk, dvt, mvt = _proj(x2d, w_qk, w_vt, seq)
        a_out = _diff_attn(dq, dk, dvt, diff_lambda[l], diff_subln_g[l][None, :], batch, seq)
        b_out = _moba_attn(mq, mk, mvt, batch, seq)
        x1 = _mix_ln(a_out, b_out, x2d, w_o[l].astype(jnp.bfloat16),
                     ln1_g[l][None, :], ln1_b[l][None, :])
        x2d = _ffn_ln(x1, w_ffn_in[l].astype(jnp.bfloat16), w_ffn_out[l].astype(jnp.bfloat16),
                      ln2_g[l][None, :], ln2_b[l][None, :])
    return x2d.reshape(batch, seq, d)
```

```python
import functools
import math

import numpy as np
import jax
import jax.numpy as jnp
from jax import lax
from jax.experimental import pallas as pl
from jax.experimental.pallas import tpu as pltpu

D_MODEL = 1024
HEAD_DIM = 64
N_DIFF_HEADS = 4
DIFF_WIDTH = N_DIFF_HEADS * 2 * HEAD_DIM
N_MOBA_HEADS = 8
MOBA_WIDTH = N_MOBA_HEADS * HEAD_DIM
ROT_DIM = HEAD_DIM // 4
ROT_HALF = ROT_DIM // 2
ROPE_THETA = 500000.0
MOBA_BLOCK = 256
MOBA_TOPK = 3
D_FF = 2816
DEPTH = 1
ALPHA = (2.0 * DEPTH) ** 0.25
LN_EPS = 1e-5
LAMBDA_INIT = 0.8 - 0.6 * math.exp(-0.3 * 0)
Q_SCALE = HEAD_DIM ** -0.5 * math.log2(math.e)

V7X_LANES = 128
V7X_SUBLANES = 8
V7X_BF16_ROWS = 16
V7X_VMEM_BYTES = 64 * 1024 * 1024
VMEM_LIMIT = V7X_VMEM_BYTES - 8 * 1024 * 1024

ATTN_TILE = 512
PROJ_ROWS = ATTN_TILE
MIX_ROWS = 512
FFN_ROWS = 256
FFN_CHUNK = 256
ONES_ROWS = V7X_BF16_ROWS
NEG_BIG = -1e30

_NT = (((1,), (1,)), ((), ()))


def _rope_tables(seq):
    inv = ROPE_THETA ** (-np.arange(0, ROT_DIM, 2, dtype=np.float64) / ROT_DIM)
    ang = np.arange(seq, dtype=np.float64)[:, None] * inv[None, :]
    cos = np.ones((seq, V7X_LANES), np.float64)
    sin = np.zeros((seq, V7X_LANES), np.float64)
    hot = np.zeros((seq, V7X_LANES), np.float64)
    for base in (0, HEAD_DIM):
        cos[:, base:base + ROT_HALF] = np.cos(ang)
        cos[:, base + ROT_HALF:base + ROT_DIM] = np.cos(ang)
        sin[:, base:base + ROT_HALF] = -np.sin(ang)
        sin[:, base + ROT_HALF:base + ROT_DIM] = np.sin(ang)
    blk = np.arange(seq) // MOBA_BLOCK
    hot[np.arange(seq), blk] = 1.0
    hot[np.arange(seq), HEAD_DIM + blk] = 1.0
    f32 = lambda a: jnp.asarray(a, jnp.float32)
    return f32(cos), f32(sin), f32(hot), f32(np.cos(ang).T), f32(np.sin(ang).T)


def _proj_kernel(x_ref, wk_ref, wt_ref, cos_ref, sin_ref, hot_ref, cost_ref, sint_ref,
                 dk_ref, mk_ref, dqt_ref, mqt_ref, dvt_ref, mvt_ref, kmean_ref,
                 *, tiles_per_seq):
    tm = x_ref.shape[0]
    t = pl.program_id(0) % tiles_per_seq
    xb = x_ref[...].astype(jnp.bfloat16)

    cos = cos_ref[...]
    sin = sin_ref[...]
    lane = lax.broadcasted_iota(jnp.int32, (tm, V7X_LANES), 1)
    first_half = (lane & (HEAD_DIM - 1)) < ROT_HALF

    def rope_rows(a):
        out = []
        for c in range(a.shape[1] // V7X_LANES):
            blk = a[:, c * V7X_LANES:(c + 1) * V7X_LANES]
            partner = jnp.where(first_half,
                                pltpu.roll(blk, V7X_LANES - ROT_HALF, 1),
                                pltpu.roll(blk, ROT_HALF, 1))
            out.append(blk * cos + partner * sin)
        return out

    dk = jnp.dot(xb, wk_ref[:, :DIFF_WIDTH], preferred_element_type=jnp.float32)
    for c, blk in enumerate(rope_rows(dk)):
        dk_ref[:, c * V7X_LANES:(c + 1) * V7X_LANES] = blk.astype(jnp.bfloat16)
    kblocks = rope_rows(jnp.dot(xb, wk_ref[:, DIFF_WIDTH:], preferred_element_type=jnp.float32))

    @pl.when(t == 0)
    def _():
        kmean_ref[...] = jnp.zeros_like(kmean_ref)

    blocks_per_tile = tm // MOBA_BLOCK
    n_blocks = kmean_ref.shape[0]
    km_row = lax.broadcasted_iota(jnp.int32, (n_blocks, V7X_LANES), 0)
    for c, blk in enumerate(kblocks):
        km_c = kmean_ref[:, c * V7X_LANES:(c + 1) * V7X_LANES]
        for r in range(blocks_per_tile):
            mean = jnp.sum(blk[r * MOBA_BLOCK:(r + 1) * MOBA_BLOCK], axis=0, keepdims=True) \
                * (1.0 / MOBA_BLOCK)
            km_c = jnp.where(km_row == blocks_per_tile * t + r, mean, km_c)
        kmean_ref[:, c * V7X_LANES:(c + 1) * V7X_LANES] = km_c

    hot = hot_ref[...]
    for h in range(N_MOBA_HEADS):
        own_lanes = (lane < HEAD_DIM) if h % 2 == 0 else (lane >= HEAD_DIM)
        k_aug = jnp.where(own_lanes, kblocks[h // 2], hot)
        mk_ref[:, h * V7X_LANES:(h + 1) * V7X_LANES] = k_aug.astype(jnp.bfloat16)

    cost = cost_ref[...]
    sint = sint_ref[...]

    def seg_t(j):
        return lax.dot_general(wt_ref[j * DIFF_WIDTH:(j + 1) * DIFF_WIDTH, :], xb, _NT,
                               preferred_element_type=jnp.float32)

    def rope_t(a):
        x1 = a[:ROT_HALF]
        x2 = a[ROT_HALF:ROT_DIM]
        return jnp.concatenate([x1 * cost - x2 * sint, x2 * cost + x1 * sint, a[ROT_DIM:]],
                               axis=0) * Q_SCALE

    dqt = seg_t(0)
    for h in range(N_DIFF_HEADS):
        maps = [rope_t(dqt[(2 * h + c) * HEAD_DIM:(2 * h + c + 1) * HEAD_DIM]) for c in range(2)]
        dqt_ref[0, h] = jnp.concatenate(maps, axis=0).astype(jnp.bfloat16)
    mqt = seg_t(1)
    q_heads = [rope_t(mqt[h * HEAD_DIM:(h + 1) * HEAD_DIM]) for h in range(N_MOBA_HEADS)]
    dvt = seg_t(2)
    for h in range(dvt_ref.shape[1]):
        dvt_ref[0, h] = dvt[h * V7X_LANES:(h + 1) * V7X_LANES].astype(jnp.bfloat16)
    mvt = seg_t(3)
    for g in range(mvt_ref.shape[1]):
        mvt_ref[0, g] = mvt[g * V7X_LANES:(g + 1) * V7X_LANES].astype(jnp.bfloat16)

    km = kmean_ref[...]
    km_rows = jnp.concatenate([km] * N_MOBA_HEADS, axis=0)
    r_id = lax.broadcasted_iota(jnp.int32, km_rows.shape, 0)
    l_id = lax.broadcasted_iota(jnp.int32, km_rows.shape, 1)
    km_rows = jnp.where((r_id // n_blocks) == (l_id // HEAD_DIM), km_rows, 0.0)
    gate_t = jnp.dot(km_rows, jnp.concatenate(q_heads, axis=0), precision=lax.Precision.HIGHEST,
                     preferred_element_type=jnp.float32)

    n_id = lax.broadcasted_iota(jnp.int32, (n_blocks, tm), 0)
    pos = lax.broadcasted_iota(jnp.int32, (n_blocks, tm), 1)
    own = blocks_per_tile * t + pos // MOBA_BLOCK
    past = n_id < own
    zeros = jnp.zeros((HEAD_DIM - n_blocks, tm), jnp.float32)
    for h in range(N_MOBA_HEADS):
        gm = jnp.where(past, gate_t[h * n_blocks:(h + 1) * n_blocks], -jnp.inf)
        rank = jnp.zeros((n_blocks, tm), jnp.int32)
        for m in range(n_blocks):
            g_m = gm[m:m + 1]
            rank = rank + jnp.where(n_id > m, jnp.where(g_m >= gm, 1, 0),
                                    jnp.where(g_m > gm, 1, 0))
        bias = jnp.where(n_id == own, 0.0,
                         jnp.where(past & (rank < MOBA_TOPK), 0.0, NEG_BIG))
        parts = [q_heads[h], bias, zeros] if h % 2 == 0 else [bias, zeros, q_heads[h]]
        mqt_ref[0, h] = jnp.concatenate(parts, axis=0).astype(jnp.bfloat16)


def _proj(x2d, w_k, w_t, seq):
    m = x2d.shape[0]
    tm = PROJ_ROWS
    tiles_per_seq = seq // tm
    n_blocks = seq // MOBA_BLOCK
    assert n_blocks * N_MOBA_HEADS == V7X_LANES and n_blocks <= HEAD_DIM
    cos, sin, hot, cost, sint = _rope_tables(seq)
    row_tile = lambda w: pl.BlockSpec((tm, w), lambda i: (i, 0))
    whole = lambda a: pl.BlockSpec(a.shape, lambda i: (0, 0))
    table = pl.BlockSpec((tm, V7X_LANES), lambda i: (i % tiles_per_seq, 0))
    table_t = pl.BlockSpec((ROT_HALF, tm), lambda i: (0, i % tiles_per_seq))
    t_tile = lambda n: pl.BlockSpec((1, n, V7X_LANES, tm), lambda i: (i, 0, 0, 0))
    bf = lambda w: jax.ShapeDtypeStruct((m, w), jnp.bfloat16)
    t_shape = lambda n: jax.ShapeDtypeStruct((m // tm, n, V7X_LANES, tm), jnp.bfloat16)
    n_dv = DIFF_WIDTH // V7X_LANES
    n_mv = MOBA_WIDTH // V7X_LANES
    return pl.pallas_call(
        functools.partial(_proj_kernel, tiles_per_seq=tiles_per_seq),
        grid=(m // tm,),
        in_specs=[row_tile(D_MODEL), whole(w_k), whole(w_t), table, table, table,
                  table_t, table_t],
        out_specs=[row_tile(DIFF_WIDTH), row_tile(2 * MOBA_WIDTH),
                   t_tile(N_DIFF_HEADS), t_tile(N_MOBA_HEADS), t_tile(n_dv), t_tile(n_mv)],
        out_shape=[bf(DIFF_WIDTH), bf(2 * MOBA_WIDTH),
                   t_shape(N_DIFF_HEADS), t_shape(N_MOBA_HEADS), t_shape(n_dv), t_shape(n_mv)],
        scratch_shapes=[pltpu.VMEM((n_blocks, MOBA_WIDTH), jnp.float32)],
        compiler_params=pltpu.CompilerParams(dimension_semantics=("arbitrary",),
                                             vmem_limit_bytes=VMEM_LIMIT),
        name="proj",
    )(x2d, w_k, w_t, cos, sin, hot, cost, sint)


def _causal_mask_t(t):
    key = lax.broadcasted_iota(jnp.int32, (t, t), 0)
    qry = lax.broadcasted_iota(jnp.int32, (t, t), 1)
    return key <= qry


def _with_ones(vt):
    return jnp.concatenate([vt, jnp.ones((ONES_ROWS, vt.shape[1]), vt.dtype)], axis=0)


def _flash_two_streams(i, t, scores, pv, s_ref, p_ref, acc_ref):
    def nth(n):
        return jnp.where(n <= 0, i, jnp.minimum(n - 1, i))

    def put_scores(n, slot, mask=None):
        s = scores(nth(n))
        for x in range(2):
            s_ref[slot, x] = s[x] if mask is None else jnp.where(mask, s[x], -jnp.inf)

    def softmax(n, slot, m):
        live = n <= i
        m_out, alpha = [], []
        for x in range(2):
            tile_max = jnp.max(s_ref[slot, x], axis=0, keepdims=True)
            m_new = jnp.maximum(m[x], jnp.where(live, tile_max, -jnp.inf))
            alpha.append(jnp.exp2(m[x] - m_new))
            p = jnp.exp2(s_ref[slot, x] - jnp.where(live, m_new, jnp.inf))
            p_ref[slot, x] = p.astype(jnp.bfloat16)
            m_out.append(m_new)
        return m_out, alpha

    def add_values(n, slot, alpha):
        o = pv(nth(n), [p_ref[slot, 0], p_ref[slot, 1]])
        for x in range(2):
            acc_ref[x] = alpha[x] * acc_ref[x] + o[x]

    row = lambda v: jnp.full((1, t), v, jnp.float32)
    put_scores(0, 0, _causal_mask_t(t))
    p_ref[1] = jnp.zeros(p_ref.shape[1:], p_ref.dtype)
    acc_ref[...] = jnp.zeros(acc_ref.shape, acc_ref.dtype)

    def two_steps(u, carry):
        m, a_prev = carry
        n = 2 * u
        put_scores(n + 1, 1)
        m, a_cur = softmax(n, 0, m)
        add_values(n - 1, 1, a_prev)
        put_scores(n + 2, 0)
        m, a_next = softmax(n + 1, 1, m)
        add_values(n, 0, a_cur)
        return m, a_next

    trips = (i + 1) // 2
    m, a_prev = lax.fori_loop(0, trips, two_steps, ([row(-jnp.inf)] * 2, [row(1.0)] * 2))
    n = 2 * trips
    m, a_last = softmax(n, 0, m)
    add_values(n - 1, 1, a_prev)
    o = pv(nth(n), [p_ref[0, 0], p_ref[0, 1]])
    return [a_last[x] * acc_ref[x] + o[x] for x in range(2)]


def _flash_scratch(t, value_rows):
    return [pltpu.VMEM((2, 2, t, t), jnp.float32), pltpu.VMEM((2, 2, t, t), jnp.bfloat16),
            pltpu.VMEM((2, value_rows + ONES_ROWS, t), jnp.float32)]


def _diff_kernel(lam_ref, g_ref, qt_ref, k_ref, vt_ref, o_ref, s_ref, p_ref, acc_ref):
    t = o_ref.shape[0]
    width = 2 * HEAD_DIM
    i = pl.program_id(2)
    qt = qt_ref[0, 0]
    feat = lax.broadcasted_iota(jnp.int32, qt.shape, 0)
    zero = jnp.zeros_like(qt)
    qs = [jnp.where(feat < HEAD_DIM, qt, zero), jnp.where(feat >= HEAD_DIM, qt, zero)]

    def scores(j):
        k = k_ref[pl.ds(pl.multiple_of(j * t, t), t), :]
        return [jnp.dot(k, qs[x], preferred_element_type=jnp.float32) for x in range(2)]

    def pv(j, p):
        lhs = _with_ones(vt_ref[j, 0])
        return [jnp.dot(lhs, p[x], preferred_element_type=jnp.float32) for x in range(2)]

    acc1, acc2 = _flash_two_streams(i, t, scores, pv, s_ref, p_ref, acc_ref)

    lp = lam_ref[...]
    lam = (jnp.exp(jnp.sum(lp[0:1] * lp[1:2], axis=1, keepdims=True))
           - jnp.exp(jnp.sum(lp[2:3] * lp[3:4], axis=1, keepdims=True)) + LAMBDA_INIT)
    o = (acc1[:width] / acc1[width:width + 1]
         - lam * (acc2[:width] / acc2[width:width + 1]))
    ms = jnp.mean(o * o, axis=0, keepdims=True)
    o = (o * lax.rsqrt(ms + LN_EPS)).T
    o = o * g_ref[...] * (1.0 - LAMBDA_INIT)
    o_ref[...] = o.astype(o_ref.dtype)


def _diff_attn(dqt, dk, dvt, lam_params, subln_g, batch, seq):
    t = ATTN_TILE
    nq = seq // t
    width = 2 * HEAD_DIM
    out_spec = pl.BlockSpec((t, width), lambda b, h, i: (b * nq + i, h))
    return pl.pallas_call(
        _diff_kernel,
        grid=(batch, N_DIFF_HEADS, nq),
        in_specs=[pl.BlockSpec(lam_params.shape, lambda b, h, i: (0, 0)),
                  pl.BlockSpec(subln_g.shape, lambda b, h, i: (0, 0)),
                  pl.BlockSpec((1, 1, width, t), lambda b, h, i: (b * nq + i, h, 0, 0)),
                  pl.BlockSpec((seq, width), lambda b, h, i: (b, h)),
                  pl.BlockSpec((nq, 1, width, t), lambda b, h, i: (b, h, 0, 0))],
        out_specs=out_spec,
        out_shape=jax.ShapeDtypeStruct(dk.shape, jnp.bfloat16),
        scratch_shapes=_flash_scratch(t, width),
        compiler_params=pltpu.CompilerParams(
            dimension_semantics=("arbitrary", "arbitrary", "arbitrary"),
            vmem_limit_bytes=VMEM_LIMIT),
        name="diff_attn",
    )(lam_params, subln_g, dqt, dk, dvt)


def _moba_kernel(qt_ref, k_ref, vt_ref, o_ref, s_ref, p_ref, acc_ref):
    t = o_ref.shape[0]
    i = pl.program_id(2)
    qs = [qt_ref[0, 0], qt_ref[0, 1]]

    def scores(j):
        k = k_ref[pl.ds(pl.multiple_of(j * t, t), t), :]
        return [jnp.dot(k[:, x * V7X_LANES:(x + 1) * V7X_LANES], qs[x],
                        preferred_element_type=jnp.float32) for x in range(2)]

    def pv(j, p):
        vt = vt_ref[j, 0]
        return [jnp.dot(_with_ones(vt[x * HEAD_DIM:(x + 1) * HEAD_DIM]), p[x],
                        preferred_element_type=jnp.float32) for x in range(2)]

    acca, accb = _flash_two_streams(i, t, scores, pv, s_ref, p_ref, acc_ref)
    o = jnp.concatenate([acca[:HEAD_DIM] / acca[HEAD_DIM:HEAD_DIM + 1],
                         accb[:HEAD_DIM] / accb[HEAD_DIM:HEAD_DIM + 1]], axis=0)
    o_ref[...] = o.T.astype(o_ref.dtype)


def _moba_attn(mqt, mk, mvt, batch, seq):
    t = ATTN_TILE
    nq = seq // t
    pairs = N_MOBA_HEADS // 2
    return pl.pallas_call(
        _moba_kernel,
        grid=(batch, pairs, nq),
        in_specs=[pl.BlockSpec((1, 2, V7X_LANES, t), lambda b, g, i: (b * nq + i, g, 0, 0)),
                  pl.BlockSpec((seq, 2 * V7X_LANES), lambda b, g, i: (b, g)),
                  pl.BlockSpec((nq, 1, V7X_LANES, t), lambda b, g, i: (b, g, 0, 0))],
        out_specs=pl.BlockSpec((t, V7X_LANES), lambda b, g, i: (b * nq + i, g)),
        out_shape=jax.ShapeDtypeStruct((batch * seq, MOBA_WIDTH), jnp.bfloat16),
        scratch_shapes=_flash_scratch(t, HEAD_DIM),
        compiler_params=pltpu.CompilerParams(
            dimension_semantics=("arbitrary", "arbitrary", "arbitrary"),
            vmem_limit_bytes=VMEM_LIMIT),
        name="moba_attn",
    )(mqt, mk, mvt)


def _layer_norm(y, g, b):
    mu = jnp.mean(y, axis=-1, keepdims=True)
    d = y - mu
    var = jnp.mean(d * d, axis=-1, keepdims=True)
    return d * lax.rsqrt(var + LN_EPS) * g + b


def _mix_ln_kernel(a_ref, b_ref, x_ref, wo_ref, g_ref, beta_ref, o_ref):
    mix = jnp.dot(a_ref[...], wo_ref[:DIFF_WIDTH, :], preferred_element_type=jnp.float32)
    mix = mix + jnp.dot(b_ref[...], wo_ref[DIFF_WIDTH:, :], preferred_element_type=jnp.float32)
    y = ALPHA * x_ref[...] + mix
    o_ref[...] = _layer_norm(y, g_ref[...], beta_ref[...])


def _mix_ln(a_out, b_out, x2d, wo_bf16, g, beta):
    m = x2d.shape[0]
    tm = MIX_ROWS
    row_tile = lambda w: pl.BlockSpec((tm, w), lambda i: (i, 0))
    whole = lambda a: pl.BlockSpec(a.shape, lambda i: (0, 0))
    return pl.pallas_call(
        _mix_ln_kernel,
        grid=(m // tm,),
        in_specs=[row_tile(DIFF_WIDTH), row_tile(MOBA_WIDTH), row_tile(D_MODEL),
                  whole(wo_bf16), whole(g), whole(beta)],
        out_specs=row_tile(D_MODEL),
        out_shape=jax.ShapeDtypeStruct(x2d.shape, jnp.float32),
        compiler_params=pltpu.CompilerParams(dimension_semantics=("arbitrary",),
                                             vmem_limit_bytes=VMEM_LIMIT),
        name="mix_ln",
    )(a_out, b_out, x2d, wo_bf16, g, beta)


def _ffn_ln_kernel(x_ref, win_ref, wout_ref, g_ref, beta_ref, o_ref, acc_ref):
    x = x_ref[...]
    xb = x.astype(jnp.bfloat16)
    for c in range(D_FF // FFN_CHUNK):
        lo = c * FFN_CHUNK
        gate = jnp.dot(xb, win_ref[:, lo:lo + FFN_CHUNK], preferred_element_type=jnp.float32)
        up = jnp.dot(xb, win_ref[:, D_FF + lo:D_FF + lo + FFN_CHUNK],
                     preferred_element_type=jnp.float32)
        h = (gate * jax.nn.sigmoid(gate) * up).astype(jnp.bfloat16)
        part = jnp.dot(h, wout_ref[lo:lo + FFN_CHUNK, :], preferred_element_type=jnp.float32)
        if c == 0:
            acc_ref[...] = part
        else:
            acc_ref[...] += part
    y = ALPHA * x + acc_ref[...]
    o_ref[...] = _layer_norm(y, g_ref[...], beta_ref[...])


def _ffn_ln(x1, win_bf16, wout_bf16, g, beta):
    m = x1.shape[0]
    tm = FFN_ROWS
    assert D_FF % FFN_CHUNK == 0
    row_tile = pl.BlockSpec((tm, D_MODEL), lambda i: (i, 0))
    whole = lambda a: pl.BlockSpec(a.shape, lambda i: (0, 0))
    return pl.pallas_call(
        _ffn_ln_kernel,
        grid=(m // tm,),
        in_specs=[row_tile, whole(win_bf16), whole(wout_bf16), whole(g), whole(beta)],
        out_specs=row_tile,
        out_shape=jax.ShapeDtypeStruct(x1.shape, jnp.float32),
        scratch_shapes=[pltpu.VMEM((tm, D_MODEL), jnp.float32)],
        compiler_params=pltpu.CompilerParams(dimension_semantics=("arbitrary",),
                                             vmem_limit_bytes=VMEM_LIMIT),
        name="ffn_ln",
    )(x1, win_bf16, wout_bf16, g, beta)


def _split_w_in(w):
    dk0, dv0, mq0 = DIFF_WIDTH, 2 * DIFF_WIDTH, 3 * DIFF_WIDTH
    mk0, mv0 = mq0 + MOBA_WIDTH, mq0 + 2 * MOBA_WIDTH
    w_k = jnp.concatenate([w[:, dk0:dv0], w[:, mk0:mv0]], axis=1)
    w_t = jnp.concatenate([w[:, :dk0], w[:, mq0:mk0], w[:, dv0:mq0], w[:, mv0:]], axis=1).T
    return w_k.astype(jnp.bfloat16), w_t.astype(jnp.bfloat16)


def kernel(x, w_in, diff_lambda, diff_subln_g, w_o, ln1_g, ln1_b, w_ffn_in, w_ffn_out, ln2_g, ln2_b):
    batch, seq, d = x.shape
    assert d == D_MODEL and seq % ATTN_TILE == 0 and w_in.shape[0] == DEPTH
    x2d = x.reshape(batch * seq, d)
    for l in range(DEPTH):
        w_k, w_t = _split_w_in(w_in[l])
        dk, mk, dqt, mqt, dvt, mvt = _proj(x2d, w_k, w_t, seq)
        a_out = _diff_attn(dqt, dk, dvt, diff_lambda[l], diff_subln_g[l][None, :], batch, seq)
        b_out = _moba_attn(mqt, mk, mvt, batch, seq)
        x1 = _mix_ln(a_out, b_out, x2d, w_o[l].astype(jnp.bfloat16),
                     ln1_g[l][None, :], ln1_b[l][None, :])
        x2d = _ffn_ln(x1, w_ffn_in[l].astype(jnp.bfloat16), w_ffn_out[l].astype(jnp.bfloat16),
                      ln2_g[l][None, :], ln2_b[l][None, :])
    return x2d.reshape(batch, seq, d)
```

```python
import functools
import math

import numpy as np
import jax
import jax.numpy as jnp
from jax import lax
from jax.experimental import pallas as pl
from jax.experimental.pallas import tpu as pltpu

D_MODEL = 1024
HEAD_DIM = 64
N_DIFF_HEADS = 4
DIFF_WIDTH = N_DIFF_HEADS * 2 * HEAD_DIM
N_MOBA_HEADS = 8
MOBA_WIDTH = N_MOBA_HEADS * HEAD_DIM
ROT_DIM = HEAD_DIM // 4
ROT_HALF = ROT_DIM // 2
ROPE_THETA = 500000.0
MOBA_BLOCK = 256
MOBA_TOPK = 3
D_FF = 2816
DEPTH = 1
ALPHA = (2.0 * DEPTH) ** 0.25
LN_EPS = 1e-5
LAMBDA_INIT = 0.8 - 0.6 * math.exp(-0.3 * 0)
Q_SCALE = HEAD_DIM ** -0.5 * math.log2(math.e)

V7X_LANES = 128
V7X_SUBLANES = 8
V7X_BF16_ROWS = 16
V7X_VMEM_BYTES = 64 * 1024 * 1024
VMEM_LIMIT = V7X_VMEM_BYTES - 8 * 1024 * 1024

ATTN_TILE = 512
PROJ_ROWS = ATTN_TILE
MIX_ROWS = 512
FFN_ROWS = 256
FFN_CHUNK = 256
ONES_ROWS = V7X_BF16_ROWS
STEP_UNROLL = 2
NEG_BIG = -1e30

_NT = (((1,), (1,)), ((), ()))


def _rope_tables(seq):
    inv = ROPE_THETA ** (-np.arange(0, ROT_DIM, 2, dtype=np.float64) / ROT_DIM)
    ang = np.arange(seq, dtype=np.float64)[:, None] * inv[None, :]
    cos = np.ones((seq, V7X_LANES), np.float64)
    sin = np.zeros((seq, V7X_LANES), np.float64)
    hot = np.zeros((seq, V7X_LANES), np.float64)
    for base in (0, HEAD_DIM):
        cos[:, base:base + ROT_HALF] = np.cos(ang)
        cos[:, base + ROT_HALF:base + ROT_DIM] = np.cos(ang)
        sin[:, base:base + ROT_HALF] = -np.sin(ang)
        sin[:, base + ROT_HALF:base + ROT_DIM] = np.sin(ang)
    blk = np.arange(seq) // MOBA_BLOCK
    hot[np.arange(seq), blk] = 1.0
    hot[np.arange(seq), HEAD_DIM + blk] = 1.0
    f32 = lambda a: jnp.asarray(a, jnp.float32)
    return f32(cos), f32(sin), f32(hot), f32(np.cos(ang).T), f32(np.sin(ang).T)


def _proj_kernel(x_ref, wk_ref, wt_ref, cos_ref, sin_ref, hot_ref, cost_ref, sint_ref,
                 dk_ref, mk_ref, dqt_ref, mqt_ref, dvt_ref, mvt_ref, kmean_ref,
                 *, tiles_per_seq):
    tm = x_ref.shape[0]
    t = pl.program_id(0) % tiles_per_seq
    xb = x_ref[...].astype(jnp.bfloat16)

    cos = cos_ref[...]
    sin = sin_ref[...]
    lane = lax.broadcasted_iota(jnp.int32, (tm, V7X_LANES), 1)
    first_half = (lane & (HEAD_DIM - 1)) < ROT_HALF

    def rope_rows(a):
        out = []
        for c in range(a.shape[1] // V7X_LANES):
            blk = a[:, c * V7X_LANES:(c + 1) * V7X_LANES]
            partner = jnp.where(first_half,
                                pltpu.roll(blk, V7X_LANES - ROT_HALF, 1),
                                pltpu.roll(blk, ROT_HALF, 1))
            out.append(blk * cos + partner * sin)
        return out

    dk = jnp.dot(xb, wk_ref[:, :DIFF_WIDTH], preferred_element_type=jnp.float32)
    for c, blk in enumerate(rope_rows(dk)):
        dk_ref[:, c * V7X_LANES:(c + 1) * V7X_LANES] = blk.astype(jnp.bfloat16)
    kblocks = rope_rows(jnp.dot(xb, wk_ref[:, DIFF_WIDTH:], preferred_element_type=jnp.float32))

    @pl.when(t == 0)
    def _():
        kmean_ref[...] = jnp.zeros_like(kmean_ref)

    blocks_per_tile = tm // MOBA_BLOCK
    n_blocks = kmean_ref.shape[0]
    km_row = lax.broadcasted_iota(jnp.int32, (n_blocks, V7X_LANES), 0)
    for c, blk in enumerate(kblocks):
        km_c = kmean_ref[:, c * V7X_LANES:(c + 1) * V7X_LANES]
        for r in range(blocks_per_tile):
            mean = jnp.sum(blk[r * MOBA_BLOCK:(r + 1) * MOBA_BLOCK], axis=0, keepdims=True) \
                * (1.0 / MOBA_BLOCK)
            km_c = jnp.where(km_row == blocks_per_tile * t + r, mean, km_c)
        kmean_ref[:, c * V7X_LANES:(c + 1) * V7X_LANES] = km_c

    hot = hot_ref[...]
    for h in range(N_MOBA_HEADS):
        own_lanes = (lane < HEAD_DIM) if h % 2 == 0 else (lane >= HEAD_DIM)
        k_aug = jnp.where(own_lanes, kblocks[h // 2], hot)
        mk_ref[:, h * V7X_LANES:(h + 1) * V7X_LANES] = k_aug.astype(jnp.bfloat16)

    cost = cost_ref[...]
    sint = sint_ref[...]

    def seg_t(j):
        return lax.dot_general(wt_ref[j * DIFF_WIDTH:(j + 1) * DIFF_WIDTH, :], xb, _NT,
                               preferred_element_type=jnp.float32)

    def rope_t(a):
        x1 = a[:ROT_HALF]
        x2 = a[ROT_HALF:ROT_DIM]
        return jnp.concatenate([x1 * cost - x2 * sint, x2 * cost + x1 * sint, a[ROT_DIM:]],
                               axis=0) * Q_SCALE

    dqt = seg_t(0)
    for h in range(N_DIFF_HEADS):
        maps = [rope_t(dqt[(2 * h + c) * HEAD_DIM:(2 * h + c + 1) * HEAD_DIM]) for c in range(2)]
        dqt_ref[0, h] = jnp.concatenate(maps, axis=0).astype(jnp.bfloat16)
    mqt = seg_t(1)
    q_heads = [rope_t(mqt[h * HEAD_DIM:(h + 1) * HEAD_DIM]) for h in range(N_MOBA_HEADS)]
    dvt = seg_t(2)
    for h in range(dvt_ref.shape[1]):
        dvt_ref[0, h] = dvt[h * V7X_LANES:(h + 1) * V7X_LANES].astype(jnp.bfloat16)
    mvt = seg_t(3)
    for g in range(mvt_ref.shape[1]):
        mvt_ref[0, g] = mvt[g * V7X_LANES:(g + 1) * V7X_LANES].astype(jnp.bfloat16)

    km = kmean_ref[...]
    km_rows = jnp.concatenate([km] * N_MOBA_HEADS, axis=0)
    r_id = lax.broadcasted_iota(jnp.int32, km_rows.shape, 0)
    l_id = lax.broadcasted_iota(jnp.int32, km_rows.shape, 1)
    km_rows = jnp.where((r_id // n_blocks) == (l_id // HEAD_DIM), km_rows, 0.0)
    gate_t = jnp.dot(km_rows, jnp.concatenate(q_heads, axis=0), precision=lax.Precision.HIGHEST,
                     preferred_element_type=jnp.float32)

    n_id = lax.broadcasted_iota(jnp.int32, (n_blocks, tm), 0)
    pos = lax.broadcasted_iota(jnp.int32, (n_blocks, tm), 1)
    own = blocks_per_tile * t + pos // MOBA_BLOCK
    past = n_id < own
    zeros = jnp.zeros((HEAD_DIM - n_blocks, tm), jnp.float32)
    for h in range(N_MOBA_HEADS):
        gm = jnp.where(past, gate_t[h * n_blocks:(h + 1) * n_blocks], -jnp.inf)
        rank = jnp.zeros((n_blocks, tm), jnp.int32)
        for m in range(n_blocks):
            g_m = gm[m:m + 1]
            rank = rank + jnp.where(n_id > m, jnp.where(g_m >= gm, 1, 0),
                                    jnp.where(g_m > gm, 1, 0))
        bias = jnp.where(n_id == own, 0.0,
                         jnp.where(past & (rank < MOBA_TOPK), 0.0, NEG_BIG))
        parts = [q_heads[h], bias, zeros] if h % 2 == 0 else [bias, zeros, q_heads[h]]
        mqt_ref[0, h] = jnp.concatenate(parts, axis=0).astype(jnp.bfloat16)


def _proj(x2d, w_k, w_t, seq):
    m = x2d.shape[0]
    tm = PROJ_ROWS
    tiles_per_seq = seq // tm
    n_blocks = seq // MOBA_BLOCK
    assert n_blocks * N_MOBA_HEADS == V7X_LANES and n_blocks <= HEAD_DIM
    cos, sin, hot, cost, sint = _rope_tables(seq)
    row_tile = lambda w: pl.BlockSpec((tm, w), lambda i: (i, 0))
    whole = lambda a: pl.BlockSpec(a.shape, lambda i: (0, 0))
    table = pl.BlockSpec((tm, V7X_LANES), lambda i: (i % tiles_per_seq, 0))
    table_t = pl.BlockSpec((ROT_HALF, tm), lambda i: (0, i % tiles_per_seq))
    t_tile = lambda n: pl.BlockSpec((1, n, V7X_LANES, tm), lambda i: (i, 0, 0, 0))
    bf = lambda w: jax.ShapeDtypeStruct((m, w), jnp.bfloat16)
    t_shape = lambda n: jax.ShapeDtypeStruct((m // tm, n, V7X_LANES, tm), jnp.bfloat16)
    n_dv = DIFF_WIDTH // V7X_LANES
    n_mv = MOBA_WIDTH // V7X_LANES
    return pl.pallas_call(
        functools.partial(_proj_kernel, tiles_per_seq=tiles_per_seq),
        grid=(m // tm,),
        in_specs=[row_tile(D_MODEL), whole(w_k), whole(w_t), table, table, table,
                  table_t, table_t],
        out_specs=[row_tile(DIFF_WIDTH), row_tile(2 * MOBA_WIDTH),
                   t_tile(N_DIFF_HEADS), t_tile(N_MOBA_HEADS), t_tile(n_dv), t_tile(n_mv)],
        out_shape=[bf(DIFF_WIDTH), bf(2 * MOBA_WIDTH),
                   t_shape(N_DIFF_HEADS), t_shape(N_MOBA_HEADS), t_shape(n_dv), t_shape(n_mv)],
        scratch_shapes=[pltpu.VMEM((n_blocks, MOBA_WIDTH), jnp.float32)],
        compiler_params=pltpu.CompilerParams(dimension_semantics=("arbitrary",),
                                             vmem_limit_bytes=VMEM_LIMIT),
        name="proj",
    )(x2d, w_k, w_t, cos, sin, hot, cost, sint)


def _causal_mask_t(t):
    key = lax.broadcasted_iota(jnp.int32, (t, t), 0)
    qry = lax.broadcasted_iota(jnp.int32, (t, t), 1)
    return key <= qry


def _with_ones(vt):
    return jnp.concatenate([vt, jnp.ones((ONES_ROWS, vt.shape[1]), vt.dtype)], axis=0)


def _step_table(nq):
    pairs = [(i, i) for i in range(nq)]
    pairs += [(i, j) for j in range(nq - 1) for i in range(j + 1, nq)]
    pairs.append(pairs[-1])
    return jnp.asarray(np.array(pairs, np.int32).T)


def _flash_all_tiles(tab_ref, nq, t, scores, pv, s_ref, p_ref, m_ref, acc_ref):
    n_steps = nq * (nq + 1) // 2
    assert nq % STEP_UNROLL == 0 and (n_steps - nq) % STEP_UNROLL == 0 and STEP_UNROLL % 2 == 0
    mask = _causal_mask_t(t)

    def put_scores(k, slot, masked):
        s = scores(tab_ref[0, k], tab_ref[1, k])
        if masked is True:
            s = [jnp.where(mask, x, -jnp.inf) for x in s]
        elif masked is not False:
            hide = jnp.where(mask, 0.0, jnp.where(masked, -jnp.inf, 0.0))
            s = [x + hide for x in s]
        for x in range(2):
            s_ref[slot, x] = s[x]

    def softmax(k, slot):
        i = tab_ref[0, k]
        alpha = []
        for x in range(2):
            m_old = m_ref[i, x, 0:1, :]
            m_new = jnp.maximum(m_old, jnp.max(s_ref[slot, x], axis=0, keepdims=True))
            alpha.append(jnp.exp2(m_old - m_new))
            p_ref[slot, x] = jnp.exp2(s_ref[slot, x] - m_new).astype(jnp.bfloat16)
            m_ref[i, x, 0:1, :] = m_new
        return alpha

    def add_values(k, slot, alpha):
        i = tab_ref[0, k]
        o = pv(tab_ref[1, k], [p_ref[slot, 0], p_ref[slot, 1]])
        for x in range(2):
            acc_ref[i, x] = alpha[x] * acc_ref[i, x] + o[x]

    def trip(first, next_masked):
        def body(u, a_prev):
            k0 = first + STEP_UNROLL * u
            for r in range(STEP_UNROLL):
                k = k0 + r
                put_scores(k + 1, (r + 1) % 2, next_masked(k + 1, r))
                a_cur = softmax(k, r % 2)
                add_values(jnp.maximum(k - 1, 0), (r + 1) % 2, a_prev)
                a_prev = a_cur
            return a_prev
        return body

    m_ref[...] = jnp.full(m_ref.shape, -jnp.inf, m_ref.dtype)
    acc_ref[...] = jnp.zeros(acc_ref.shape, acc_ref.dtype)
    p_ref[1] = jnp.zeros(p_ref.shape[1:], p_ref.dtype)
    put_scores(0, 0, True)
    ones = [jnp.ones((1, t), jnp.float32)] * 2
    diag_next = lambda k, r: True if r < STEP_UNROLL - 1 else k < nq
    a_prev = lax.fori_loop(0, nq // STEP_UNROLL, trip(0, diag_next), ones)
    a_prev = lax.fori_loop(0, (n_steps - nq) // STEP_UNROLL, trip(nq, lambda k, r: False), a_prev)
    add_values(n_steps - 1, (n_steps - 1) % 2, a_prev)


def _flash_scratch(nq, t, value_rows):
    return [pltpu.VMEM((2, 2, t, t), jnp.float32), pltpu.VMEM((2, 2, t, t), jnp.bfloat16),
            pltpu.VMEM((nq, 2, V7X_SUBLANES, t), jnp.float32),
            pltpu.VMEM((nq, 2, value_rows + ONES_ROWS, t), jnp.float32)]


def _flash_grid_spec(grid, in_specs, out_specs, nq, t, value_rows):
    return pltpu.PrefetchScalarGridSpec(
        num_scalar_prefetch=1, grid=grid, in_specs=in_specs, out_specs=out_specs,
        scratch_shapes=_flash_scratch(nq, t, value_rows))


def _diff_kernel(tab_ref, lam_ref, g_ref, qt_ref, k_ref, vt_ref, o_ref,
                 s_ref, p_ref, m_ref, acc_ref):
    nq, _, width, t = qt_ref.shape
    feat = lax.broadcasted_iota(jnp.int32, (width, t), 0)

    def scores(i, j):
        qt = qt_ref[i, 0]
        zero = jnp.zeros_like(qt)
        k = k_ref[pl.ds(pl.multiple_of(j * t, t), t), :]
        return [jnp.dot(k, jnp.where(feat < HEAD_DIM, qt, zero),
                        preferred_element_type=jnp.float32),
                jnp.dot(k, jnp.where(feat >= HEAD_DIM, qt, zero),
                        preferred_element_type=jnp.float32)]

    def pv(j, p):
        lhs = _with_ones(vt_ref[j, 0])
        return [jnp.dot(lhs, p[x], preferred_element_type=jnp.float32) for x in range(2)]

    _flash_all_tiles(tab_ref, nq, t, scores, pv, s_ref, p_ref, m_ref, acc_ref)

    lp = lam_ref[...]
    lam = (jnp.exp(jnp.sum(lp[0:1] * lp[1:2], axis=1, keepdims=True))
           - jnp.exp(jnp.sum(lp[2:3] * lp[3:4], axis=1, keepdims=True)) + LAMBDA_INIT)
    gain = g_ref[...] * (1.0 - LAMBDA_INIT)

    def finish(i, carry):
        acc1 = acc_ref[i, 0]
        acc2 = acc_ref[i, 1]
        o = (acc1[:width] / acc1[width:width + 1]
             - lam * (acc2[:width] / acc2[width:width + 1]))
        ms = jnp.mean(o * o, axis=0, keepdims=True)
        o = (o * lax.rsqrt(ms + LN_EPS)).T * gain
        o_ref[pl.ds(pl.multiple_of(i * t, t), t), :] = o.astype(o_ref.dtype)
        return carry

    lax.fori_loop(0, nq, finish, 0)


def _diff_attn(dqt, dk, dvt, lam_params, subln_g, batch, seq):
    t = ATTN_TILE
    nq = seq // t
    width = 2 * HEAD_DIM
    tab = _step_table(nq)
    tile_spec = pl.BlockSpec((nq, 1, width, t), lambda b, h, tab: (b, h, 0, 0))
    row_spec = pl.BlockSpec((seq, width), lambda b, h, tab: (b, h))
    whole = lambda a: pl.BlockSpec(a.shape, lambda b, h, tab: (0, 0))
    return pl.pallas_call(
        _diff_kernel,
        grid_spec=_flash_grid_spec(
            (batch, N_DIFF_HEADS),
            [whole(lam_params), whole(subln_g), tile_spec, row_spec, tile_spec],
            row_spec, nq, t, width),
        out_shape=jax.ShapeDtypeStruct(dk.shape, jnp.bfloat16),
        compiler_params=pltpu.CompilerParams(
            dimension_semantics=("arbitrary", "arbitrary"), vmem_limit_bytes=VMEM_LIMIT),
        name="diff_attn",
    )(tab, lam_params, subln_g, dqt, dk, dvt)


def _moba_kernel(tab_ref, qt_ref, k_ref, vt_ref, o_ref, s_ref, p_ref, m_ref, acc_ref):
    nq, _, _, t = qt_ref.shape

    def scores(i, j):
        k = k_ref[pl.ds(pl.multiple_of(j * t, t), t), :]
        return [jnp.dot(k[:, x * V7X_LANES:(x + 1) * V7X_LANES], qt_ref[i, x],
                        preferred_element_type=jnp.float32) for x in range(2)]

    def pv(j, p):
        vt = vt_ref[j, 0]
        return [jnp.dot(_with_ones(vt[x * HEAD_DIM:(x + 1) * HEAD_DIM]), p[x],
                        preferred_element_type=jnp.float32) for x in range(2)]

    _flash_all_tiles(tab_ref, nq, t, scores, pv, s_ref, p_ref, m_ref, acc_ref)

    def finish(i, carry):
        acca = acc_ref[i, 0]
        accb = acc_ref[i, 1]
        o = jnp.concatenate([acca[:HEAD_DIM] / acca[HEAD_DIM:HEAD_DIM + 1],
                             accb[:HEAD_DIM] / accb[HEAD_DIM:HEAD_DIM + 1]], axis=0)
        o_ref[pl.ds(pl.multiple_of(i * t, t), t), :] = o.T.astype(o_ref.dtype)
        return carry

    lax.fori_loop(0, nq, finish, 0)


def _moba_attn(mqt, mk, mvt, batch, seq):
    t = ATTN_TILE
    nq = seq // t
    pairs = N_MOBA_HEADS // 2
    tab = _step_table(nq)
    return pl.pallas_call(
        _moba_kernel,
        grid_spec=_flash_grid_spec(
            (batch, pairs),
            [pl.BlockSpec((nq, 2, V7X_LANES, t), lambda b, g, tab: (b, g, 0, 0)),
             pl.BlockSpec((seq, 2 * V7X_LANES), lambda b, g, tab: (b, g)),
             pl.BlockSpec((nq, 1, V7X_LANES, t), lambda b, g, tab: (b, g, 0, 0))],
            pl.BlockSpec((seq, V7X_LANES), lambda b, g, tab: (b, g)), nq, t, HEAD_DIM),
        out_shape=jax.ShapeDtypeStruct((batch * seq, MOBA_WIDTH), jnp.bfloat16),
        compiler_params=pltpu.CompilerParams(
            dimension_semantics=("arbitrary", "arbitrary"), vmem_limit_bytes=VMEM_LIMIT),
        name="moba_attn",
    )(tab, mqt, mk, mvt)


def _layer_norm(y, g, b):
    mu = jnp.mean(y, axis=-1, keepdims=True)
    d = y - mu
    var = jnp.mean(d * d, axis=-1, keepdims=True)
    return d * lax.rsqrt(var + LN_EPS) * g + b


def _mix_ln_kernel(a_ref, b_ref, x_ref, wo_ref, g_ref, beta_ref, o_ref):
    mix = jnp.dot(a_ref[...], wo_ref[:DIFF_WIDTH, :], preferred_element_type=jnp.float32)
    mix = mix + jnp.dot(b_ref[...], wo_ref[DIFF_WIDTH:, :], preferred_element_type=jnp.float32)
    y = ALPHA * x_ref[...] + mix
    o_ref[...] = _layer_norm(y, g_ref[...], beta_ref[...])


def _mix_ln(a_out, b_out, x2d, wo_bf16, g, beta):
    m = x2d.shape[0]
    tm = MIX_ROWS
    row_tile = lambda w: pl.BlockSpec((tm, w), lambda i: (i, 0))
    whole = lambda a: pl.BlockSpec(a.shape, lambda i: (0, 0))
    return pl.pallas_call(
        _mix_ln_kernel,
        grid=(m // tm,),
        in_specs=[row_tile(DIFF_WIDTH), row_tile(MOBA_WIDTH), row_tile(D_MODEL),
                  whole(wo_bf16), whole(g), whole(beta)],
        out_specs=row_tile(D_MODEL),
        out_shape=jax.ShapeDtypeStruct(x2d.shape, jnp.float32),
        compiler_params=pltpu.CompilerParams(dimension_semantics=("arbitrary",),
                                             vmem_limit_bytes=VMEM_LIMIT),
        name="mix_ln",
    )(a_out, b_out, x2d, wo_bf16, g, beta)


def _ffn_ln_kernel(x_ref, win_ref, wout_ref, g_ref, beta_ref, o_ref, acc_ref):
    x = x_ref[...]
    xb = x.astype(jnp.bfloat16)
    for c in range(D_FF // FFN_CHUNK):
        lo = c * FFN_CHUNK
        gate = jnp.dot(xb, win_ref[:, lo:lo + FFN_CHUNK], preferred_element_type=jnp.float32)
        up = jnp.dot(xb, win_ref[:, D_FF + lo:D_FF + lo + FFN_CHUNK],
                     preferred_element_type=jnp.float32)
        h = (gate * jax.nn.sigmoid(gate) * up).astype(jnp.bfloat16)
        part = jnp.dot(h, wout_ref[lo:lo + FFN_CHUNK, :], preferred_element_type=jnp.float32)
        if c == 0:
            acc_ref[...] = part
        else:
            acc_ref[...] += part
    y = ALPHA * x + acc_ref[...]
    o_ref[...] = _layer_norm(y, g_ref[...], beta_ref[...])


def _ffn_ln(x1, win_bf16, wout_bf16, g, beta):
    m = x1.shape[0]
    tm = FFN_ROWS
    assert D_FF % FFN_CHUNK == 0
    row_tile = pl.BlockSpec((tm, D_MODEL), lambda i: (i, 0))
    whole = lambda a: pl.BlockSpec(a.shape, lambda i: (0, 0))
    return pl.pallas_call(
        _ffn_ln_kernel,
        grid=(m // tm,),
        in_specs=[row_tile, whole(win_bf16), whole(wout_bf16), whole(g), whole(beta)],
        out_specs=row_tile,
        out_shape=jax.ShapeDtypeStruct(x1.shape, jnp.float32),
        scratch_shapes=[pltpu.VMEM((tm, D_MODEL), jnp.float32)],
        compiler_params=pltpu.CompilerParams(dimension_semantics=("arbitrary",),
                                             vmem_limit_bytes=VMEM_LIMIT),
        name="ffn_ln",
    )(x1, win_bf16, wout_bf16, g, beta)


def _split_w_in(w):
    dk0, dv0, mq0 = DIFF_WIDTH, 2 * DIFF_WIDTH, 3 * DIFF_WIDTH
    mk0, mv0 = mq0 + MOBA_WIDTH, mq0 + 2 * MOBA_WIDTH
    w_k = jnp.concatenate([w[:, dk0:dv0], w[:, mk0:mv0]], axis=1)
    w_t = jnp.concatenate([w[:, :dk0], w[:, mq0:mk0], w[:, dv0:mq0], w[:, mv0:]], axis=1).T
    return w_k.astype(jnp.bfloat16), w_t.astype(jnp.bfloat16)


def kernel(x, w_in, diff_lambda, diff_subln_g, w_o, ln1_g, ln1_b, w_ffn_in, w_ffn_out, ln2_g, ln2_b):
    batch, seq, d = x.shape
    assert d == D_MODEL and seq % ATTN_TILE == 0 and w_in.shape[0] == DEPTH
    x2d = x.reshape(batch * seq, d)
    for l in range(DEPTH):
        w_k, w_t = _split_w_in(w_in[l])
        dk, mk, dqt, mqt, dvt, mvt = _proj(x2d, w_k, w_t, seq)
        a_out = _diff_attn(dqt, dk, dvt, diff_lambda[l], diff_subln_g[l][None, :], batch, seq)
        b_out = _moba_attn(mqt, mk, mvt, batch, seq)
        x1 = _mix_ln(a_out, b_out, x2d, w_o[l].astype(jnp.bfloat16),
                     ln1_g[l][None, :], ln1_b[l][None, :])
        x2d = _ffn_ln(x1, w_ffn_in[l].astype(jnp.bfloat16), w_ffn_out[l].astype(jnp.bfloat16),
                      ln2_g[l][None, :], ln2_b[l][None, :])
    return x2d.reshape(batch, seq, d)
```

```python
import functools
import math

import numpy as np
import jax
import jax.numpy as jnp
from jax import lax
from jax.experimental import pallas as pl
from jax.experimental.pallas import tpu as pltpu

D_MODEL = 1024
HEAD_DIM = 64
N_DIFF_HEADS = 4
DIFF_WIDTH = N_DIFF_HEADS * 2 * HEAD_DIM
N_MOBA_HEADS = 8
MOBA_WIDTH = N_MOBA_HEADS * HEAD_DIM
ROT_DIM = HEAD_DIM // 4
ROT_HALF = ROT_DIM // 2
ROPE_THETA = 500000.0
MOBA_BLOCK = 256
MOBA_TOPK = 3
D_FF = 2816
DEPTH = 1
ALPHA = (2.0 * DEPTH) ** 0.25
LN_EPS = 1e-5
LAMBDA_INIT = 0.8 - 0.6 * math.exp(-0.3 * 0)
Q_SCALE = HEAD_DIM ** -0.5 * math.log2(math.e)

V7X_LANES = 128
V7X_SUBLANES = 8
V7X_BF16_ROWS = 16
V7X_VMEM_BYTES = 64 * 1024 * 1024
VMEM_LIMIT = V7X_VMEM_BYTES - 8 * 1024 * 1024

ATTN_TILE = 512
PROJ_ROWS = ATTN_TILE
TAIL_ROWS = 512
FFN_CHUNK = 256
ONES_ROWS = V7X_BF16_ROWS
STEP_UNROLL = 2
NEG_BIG = -1e30

_NT = (((1,), (1,)), ((), ()))


def _rope_tables(seq):
    inv = ROPE_THETA ** (-np.arange(0, ROT_DIM, 2, dtype=np.float64) / ROT_DIM)
    ang = np.arange(seq, dtype=np.float64)[:, None] * inv[None, :]
    cos = np.ones((seq, V7X_LANES), np.float64)
    sin = np.zeros((seq, V7X_LANES), np.float64)
    hot = np.zeros((seq, V7X_LANES), np.float64)
    for base in (0, HEAD_DIM):
        cos[:, base:base + ROT_HALF] = np.cos(ang)
        cos[:, base + ROT_HALF:base + ROT_DIM] = np.cos(ang)
        sin[:, base:base + ROT_HALF] = -np.sin(ang)
        sin[:, base + ROT_HALF:base + ROT_DIM] = np.sin(ang)
    blk = np.arange(seq) // MOBA_BLOCK
    hot[np.arange(seq), blk] = 1.0
    hot[np.arange(seq), HEAD_DIM + blk] = 1.0
    f32 = lambda a: jnp.asarray(a, jnp.float32)
    return f32(cos), f32(sin), f32(hot), f32(np.cos(ang).T), f32(np.sin(ang).T)


def _proj_kernel(x_ref, wk_ref, wt_ref, cos_ref, sin_ref, hot_ref, cost_ref, sint_ref,
                 dk_ref, mk_ref, dqt_ref, mqt_ref, dvt_ref, mvt_ref, kmean_ref,
                 *, tiles_per_seq):
    tm = x_ref.shape[0]
    t = pl.program_id(0) % tiles_per_seq
    xb = x_ref[...].astype(jnp.bfloat16)

    cos = cos_ref[...]
    sin = sin_ref[...]
    lane = lax.broadcasted_iota(jnp.int32, (tm, V7X_LANES), 1)
    first_half = (lane & (HEAD_DIM - 1)) < ROT_HALF

    def rope_rows(a):
        out = []
        for c in range(a.shape[1] // V7X_LANES):
            blk = a[:, c * V7X_LANES:(c + 1) * V7X_LANES]
            partner = jnp.where(first_half,
                                pltpu.roll(blk, V7X_LANES - ROT_HALF, 1),
                                pltpu.roll(blk, ROT_HALF, 1))
            out.append(blk * cos + partner * sin)
        return out

    dk = jnp.dot(xb, wk_ref[:, :DIFF_WIDTH], preferred_element_type=jnp.float32)
    for c, blk in enumerate(rope_rows(dk)):
        dk_ref[:, c * V7X_LANES:(c + 1) * V7X_LANES] = blk.astype(jnp.bfloat16)
    kblocks = rope_rows(jnp.dot(xb, wk_ref[:, DIFF_WIDTH:], preferred_element_type=jnp.float32))

    @pl.when(t == 0)
    def _():
        kmean_ref[...] = jnp.zeros_like(kmean_ref)

    blocks_per_tile = tm // MOBA_BLOCK
    n_blocks = kmean_ref.shape[0]
    km_row = lax.broadcasted_iota(jnp.int32, (n_blocks, V7X_LANES), 0)
    for c, blk in enumerate(kblocks):
        km_c = kmean_ref[:, c * V7X_LANES:(c + 1) * V7X_LANES]
        for r in range(blocks_per_tile):
            mean = jnp.sum(blk[r * MOBA_BLOCK:(r + 1) * MOBA_BLOCK], axis=0, keepdims=True) \
                * (1.0 / MOBA_BLOCK)
            km_c = jnp.where(km_row == blocks_per_tile * t + r, mean, km_c)
        kmean_ref[:, c * V7X_LANES:(c + 1) * V7X_LANES] = km_c

    hot = hot_ref[...]
    for h in range(N_MOBA_HEADS):
        own_lanes = (lane < HEAD_DIM) if h % 2 == 0 else (lane >= HEAD_DIM)
        k_aug = jnp.where(own_lanes, kblocks[h // 2], hot)
        mk_ref[:, h * V7X_LANES:(h + 1) * V7X_LANES] = k_aug.astype(jnp.bfloat16)

    cost = cost_ref[...]
    sint = sint_ref[...]

    def seg_t(j):
        return lax.dot_general(wt_ref[j * DIFF_WIDTH:(j + 1) * DIFF_WIDTH, :], xb, _NT,
                               preferred_element_type=jnp.float32)

    def rope_t(a):
        x1 = a[:ROT_HALF]
        x2 = a[ROT_HALF:ROT_DIM]
        return jnp.concatenate([x1 * cost - x2 * sint, x2 * cost + x1 * sint, a[ROT_DIM:]],
                               axis=0) * Q_SCALE

    dqt = seg_t(0)
    for h in range(N_DIFF_HEADS):
        maps = [rope_t(dqt[(2 * h + c) * HEAD_DIM:(2 * h + c + 1) * HEAD_DIM]) for c in range(2)]
        dqt_ref[0, h] = jnp.concatenate(maps, axis=0).astype(jnp.bfloat16)
    mqt = seg_t(1)
    q_heads = [rope_t(mqt[h * HEAD_DIM:(h + 1) * HEAD_DIM]) for h in range(N_MOBA_HEADS)]
    dvt = seg_t(2)
    for h in range(dvt_ref.shape[1]):
        dvt_ref[0, h] = dvt[h * V7X_LANES:(h + 1) * V7X_LANES].astype(jnp.bfloat16)
    mvt = seg_t(3)
    for g in range(mvt_ref.shape[1]):
        mvt_ref[0, g] = mvt[g * V7X_LANES:(g + 1) * V7X_LANES].astype(jnp.bfloat16)

    km = kmean_ref[...]
    km_rows = jnp.concatenate([km] * N_MOBA_HEADS, axis=0)
    r_id = lax.broadcasted_iota(jnp.int32, km_rows.shape, 0)
    l_id = lax.broadcasted_iota(jnp.int32, km_rows.shape, 1)
    km_rows = jnp.where((r_id // n_blocks) == (l_id // HEAD_DIM), km_rows, 0.0)
    gate_t = jnp.dot(km_rows, jnp.concatenate(q_heads, axis=0), precision=lax.Precision.HIGHEST,
                     preferred_element_type=jnp.float32)

    n_id = lax.broadcasted_iota(jnp.int32, (n_blocks, tm), 0)
    pos = lax.broadcasted_iota(jnp.int32, (n_blocks, tm), 1)
    own = blocks_per_tile * t + pos // MOBA_BLOCK
    past = n_id < own
    zeros = jnp.zeros((HEAD_DIM - n_blocks, tm), jnp.float32)
    for h in range(N_MOBA_HEADS):
        gm = jnp.where(past, gate_t[h * n_blocks:(h + 1) * n_blocks], -jnp.inf)
        rank = jnp.zeros((n_blocks, tm), jnp.int32)
        for m in range(n_blocks):
            g_m = gm[m:m + 1]
            rank = rank + jnp.where(n_id > m, jnp.where(g_m >= gm, 1, 0),
                                    jnp.where(g_m > gm, 1, 0))
        bias = jnp.where(n_id == own, 0.0,
                         jnp.where(past & (rank < MOBA_TOPK), 0.0, NEG_BIG))
        parts = [q_heads[h], bias, zeros] if h % 2 == 0 else [bias, zeros, q_heads[h]]
        mqt_ref[0, h] = jnp.concatenate(parts, axis=0).astype(jnp.bfloat16)


def _proj(x2d, w_k, w_t, seq):
    m = x2d.shape[0]
    tm = PROJ_ROWS
    tiles_per_seq = seq // tm
    n_blocks = seq // MOBA_BLOCK
    assert n_blocks * N_MOBA_HEADS == V7X_LANES and n_blocks <= HEAD_DIM
    cos, sin, hot, cost, sint = _rope_tables(seq)
    row_tile = lambda w: pl.BlockSpec((tm, w), lambda i: (i, 0))
    whole = lambda a: pl.BlockSpec(a.shape, lambda i: (0, 0))
    table = pl.BlockSpec((tm, V7X_LANES), lambda i: (i % tiles_per_seq, 0))
    table_t = pl.BlockSpec((ROT_HALF, tm), lambda i: (0, i % tiles_per_seq))
    t_tile = lambda n: pl.BlockSpec((1, n, V7X_LANES, tm), lambda i: (i, 0, 0, 0))
    bf = lambda w: jax.ShapeDtypeStruct((m, w), jnp.bfloat16)
    t_shape = lambda n: jax.ShapeDtypeStruct((m // tm, n, V7X_LANES, tm), jnp.bfloat16)
    n_dv = DIFF_WIDTH // V7X_LANES
    n_mv = MOBA_WIDTH // V7X_LANES
    return pl.pallas_call(
        functools.partial(_proj_kernel, tiles_per_seq=tiles_per_seq),
        grid=(m // tm,),
        in_specs=[row_tile(D_MODEL), whole(w_k), whole(w_t), table, table, table,
                  table_t, table_t],
        out_specs=[row_tile(DIFF_WIDTH), row_tile(2 * MOBA_WIDTH),
                   t_tile(N_DIFF_HEADS), t_tile(N_MOBA_HEADS), t_tile(n_dv), t_tile(n_mv)],
        out_shape=[bf(DIFF_WIDTH), bf(2 * MOBA_WIDTH),
                   t_shape(N_DIFF_HEADS), t_shape(N_MOBA_HEADS), t_shape(n_dv), t_shape(n_mv)],
        scratch_shapes=[pltpu.VMEM((n_blocks, MOBA_WIDTH), jnp.float32)],
        compiler_params=pltpu.CompilerParams(dimension_semantics=("arbitrary",),
                                             vmem_limit_bytes=VMEM_LIMIT),
        name="proj",
    )(x2d, w_k, w_t, cos, sin, hot, cost, sint)


def _causal_mask_t(t):
    key = lax.broadcasted_iota(jnp.int32, (t, t), 0)
    qry = lax.broadcasted_iota(jnp.int32, (t, t), 1)
    return key <= qry


def _with_ones(vt):
    return jnp.concatenate([vt, jnp.ones((ONES_ROWS, vt.shape[1]), vt.dtype)], axis=0)


def _step_table(nq):
    pairs = [(i, i) for i in range(nq)]
    pairs += [(i, j) for j in range(nq - 1) for i in range(j + 1, nq)]
    pairs.append(pairs[-1])
    return jnp.asarray(np.array(pairs, np.int32).T)


def _flash_all_tiles(tab_ref, nq, t, scores, pv, s_ref, p_ref, m_ref, acc_ref):
    n_steps = nq * (nq + 1) // 2
    assert nq % STEP_UNROLL == 0 and (n_steps - nq) % STEP_UNROLL == 0 and STEP_UNROLL % 2 == 0
    mask = _causal_mask_t(t)

    def put_scores(k, slot, masked):
        s = scores(tab_ref[0, k], tab_ref[1, k])
        if masked is True:
            s = [jnp.where(mask, x, -jnp.inf) for x in s]
        elif masked is not False:
            hide = jnp.where(mask, 0.0, jnp.where(masked, -jnp.inf, 0.0))
            s = [x + hide for x in s]
        for x in range(2):
            s_ref[slot, x] = s[x]

    def softmax(k, slot):
        i = tab_ref[0, k]
        alpha = []
        for x in range(2):
            m_old = m_ref[i, x, 0:1, :]
            m_new = jnp.maximum(m_old, jnp.max(s_ref[slot, x], axis=0, keepdims=True))
            alpha.append(jnp.exp2(m_old - m_new))
            p_ref[slot, x] = jnp.exp2(s_ref[slot, x] - m_new).astype(jnp.bfloat16)
            m_ref[i, x, 0:1, :] = m_new
        return alpha

    def add_values(k, slot, alpha):
        i = tab_ref[0, k]
        o = pv(tab_ref[1, k], [p_ref[slot, 0], p_ref[slot, 1]])
        for x in range(2):
            acc_ref[i, x] = alpha[x] * acc_ref[i, x] + o[x]

    def trip(first, next_masked):
        def body(u, a_prev):
            k0 = first + STEP_UNROLL * u
            for r in range(STEP_UNROLL):
                k = k0 + r
                put_scores(k + 1, (r + 1) % 2, next_masked(k + 1, r))
                a_cur = softmax(k, r % 2)
                add_values(jnp.maximum(k - 1, 0), (r + 1) % 2, a_prev)
                a_prev = a_cur
            return a_prev
        return body

    m_ref[...] = jnp.full(m_ref.shape, -jnp.inf, m_ref.dtype)
    acc_ref[...] = jnp.zeros(acc_ref.shape, acc_ref.dtype)
    p_ref[1] = jnp.zeros(p_ref.shape[1:], p_ref.dtype)
    put_scores(0, 0, True)
    ones = [jnp.ones((1, t), jnp.float32)] * 2
    diag_next = lambda k, r: True if r < STEP_UNROLL - 1 else k < nq
    a_prev = lax.fori_loop(0, nq // STEP_UNROLL, trip(0, diag_next), ones)
    a_prev = lax.fori_loop(0, (n_steps - nq) // STEP_UNROLL, trip(nq, lambda k, r: False), a_prev)
    add_values(n_steps - 1, (n_steps - 1) % 2, a_prev)


def _flash_scratch(nq, t, value_rows):
    return [pltpu.VMEM((2, 2, t, t), jnp.float32), pltpu.VMEM((2, 2, t, t), jnp.bfloat16),
            pltpu.VMEM((nq, 2, V7X_SUBLANES, t), jnp.float32),
            pltpu.VMEM((nq, 2, value_rows + ONES_ROWS, t), jnp.float32)]


def _flash_grid_spec(grid, in_specs, out_specs, nq, t, value_rows):
    return pltpu.PrefetchScalarGridSpec(
        num_scalar_prefetch=1, grid=grid, in_specs=in_specs, out_specs=out_specs,
        scratch_shapes=_flash_scratch(nq, t, value_rows))


def _diff_kernel(tab_ref, lam_ref, g_ref, qt_ref, k_ref, vt_ref, o_ref,
                 s_ref, p_ref, m_ref, acc_ref):
    nq, _, width, t = qt_ref.shape
    feat = lax.broadcasted_iota(jnp.int32, (width, t), 0)

    def scores(i, j):
        qt = qt_ref[i, 0]
        zero = jnp.zeros_like(qt)
        k = k_ref[pl.ds(pl.multiple_of(j * t, t), t), :]
        return [jnp.dot(k, jnp.where(feat < HEAD_DIM, qt, zero),
                        preferred_element_type=jnp.float32),
                jnp.dot(k, jnp.where(feat >= HEAD_DIM, qt, zero),
                        preferred_element_type=jnp.float32)]

    def pv(j, p):
        lhs = _with_ones(vt_ref[j, 0])
        return [jnp.dot(lhs, p[x], preferred_element_type=jnp.float32) for x in range(2)]

    _flash_all_tiles(tab_ref, nq, t, scores, pv, s_ref, p_ref, m_ref, acc_ref)

    lp = lam_ref[...]
    lam = (jnp.exp(jnp.sum(lp[0:1] * lp[1:2], axis=1, keepdims=True))
           - jnp.exp(jnp.sum(lp[2:3] * lp[3:4], axis=1, keepdims=True)) + LAMBDA_INIT)
    gain = g_ref[...] * (1.0 - LAMBDA_INIT)

    def finish(i, carry):
        acc1 = acc_ref[i, 0]
        acc2 = acc_ref[i, 1]
        o = (acc1[:width] / acc1[width:width + 1]
             - lam * (acc2[:width] / acc2[width:width + 1]))
        ms = jnp.mean(o * o, axis=0, keepdims=True)
        o = (o * lax.rsqrt(ms + LN_EPS)).T * gain
        o_ref[pl.ds(pl.multiple_of(i * t, t), t), :] = o.astype(o_ref.dtype)
        return carry

    lax.fori_loop(0, nq, finish, 0)


def _diff_attn(dqt, dk, dvt, lam_params, subln_g, batch, seq):
    t = ATTN_TILE
    nq = seq // t
    width = 2 * HEAD_DIM
    tab = _step_table(nq)
    tile_spec = pl.BlockSpec((nq, 1, width, t), lambda b, h, tab: (b, h, 0, 0))
    row_spec = pl.BlockSpec((seq, width), lambda b, h, tab: (b, h))
    whole = lambda a: pl.BlockSpec(a.shape, lambda b, h, tab: (0, 0))
    return pl.pallas_call(
        _diff_kernel,
        grid_spec=_flash_grid_spec(
            (batch, N_DIFF_HEADS),
            [whole(lam_params), whole(subln_g), tile_spec, row_spec, tile_spec],
            row_spec, nq, t, width),
        out_shape=jax.ShapeDtypeStruct(dk.shape, jnp.bfloat16),
        compiler_params=pltpu.CompilerParams(
            dimension_semantics=("arbitrary", "arbitrary"), vmem_limit_bytes=VMEM_LIMIT),
        name="diff_attn",
    )(tab, lam_params, subln_g, dqt, dk, dvt)


def _moba_kernel(tab_ref, qt_ref, k_ref, vt_ref, o_ref, s_ref, p_ref, m_ref, acc_ref):
    nq, _, _, t = qt_ref.shape

    def scores(i, j):
        k = k_ref[pl.ds(pl.multiple_of(j * t, t), t), :]
        return [jnp.dot(k[:, x * V7X_LANES:(x + 1) * V7X_LANES], qt_ref[i, x],
                        preferred_element_type=jnp.float32) for x in range(2)]

    def pv(j, p):
        vt = vt_ref[j, 0]
        return [jnp.dot(_with_ones(vt[x * HEAD_DIM:(x + 1) * HEAD_DIM]), p[x],
                        preferred_element_type=jnp.float32) for x in range(2)]

    _flash_all_tiles(tab_ref, nq, t, scores, pv, s_ref, p_ref, m_ref, acc_ref)

    def finish(i, carry):
        acca = acc_ref[i, 0]
        accb = acc_ref[i, 1]
        o = jnp.concatenate([acca[:HEAD_DIM] / acca[HEAD_DIM:HEAD_DIM + 1],
                             accb[:HEAD_DIM] / accb[HEAD_DIM:HEAD_DIM + 1]], axis=0)
        o_ref[pl.ds(pl.multiple_of(i * t, t), t), :] = o.T.astype(o_ref.dtype)
        return carry

    lax.fori_loop(0, nq, finish, 0)


def _moba_attn(mqt, mk, mvt, batch, seq):
    t = ATTN_TILE
    nq = seq // t
    pairs = N_MOBA_HEADS // 2
    tab = _step_table(nq)
    return pl.pallas_call(
        _moba_kernel,
        grid_spec=_flash_grid_spec(
            (batch, pairs),
            [pl.BlockSpec((nq, 2, V7X_LANES, t), lambda b, g, tab: (b, g, 0, 0)),
             pl.BlockSpec((seq, 2 * V7X_LANES), lambda b, g, tab: (b, g)),
             pl.BlockSpec((nq, 1, V7X_LANES, t), lambda b, g, tab: (b, g, 0, 0))],
            pl.BlockSpec((seq, V7X_LANES), lambda b, g, tab: (b, g)), nq, t, HEAD_DIM),
        out_shape=jax.ShapeDtypeStruct((batch * seq, MOBA_WIDTH), jnp.bfloat16),
        compiler_params=pltpu.CompilerParams(
            dimension_semantics=("arbitrary", "arbitrary"), vmem_limit_bytes=VMEM_LIMIT),
        name="moba_attn",
    )(tab, mqt, mk, mvt)


def _layer_norm(y, g, b):
    mu = jnp.mean(y, axis=-1, keepdims=True)
    d = y - mu
    var = jnp.mean(d * d, axis=-1, keepdims=True)
    return d * lax.rsqrt(var + LN_EPS) * g + b


def _tail_kernel(a_ref, b_ref, x_ref, wo_ref, g1_ref, b1_ref, win_ref, wout_ref, g2_ref, b2_ref,
                 o_ref, acc_ref):
    mix = jnp.dot(a_ref[...], wo_ref[:DIFF_WIDTH, :], preferred_element_type=jnp.float32)
    mix = mix + jnp.dot(b_ref[...], wo_ref[DIFF_WIDTH:, :], preferred_element_type=jnp.float32)
    x1 = _layer_norm(ALPHA * x_ref[...] + mix, g1_ref[...], b1_ref[...])
    xb = x1.astype(jnp.bfloat16)
    for c in range(D_FF // FFN_CHUNK):
        lo = c * FFN_CHUNK
        gate = jnp.dot(xb, win_ref[:, lo:lo + FFN_CHUNK], preferred_element_type=jnp.float32)
        up = jnp.dot(xb, win_ref[:, D_FF + lo:D_FF + lo + FFN_CHUNK],
                     preferred_element_type=jnp.float32)
        h = (gate * jax.nn.sigmoid(gate) * up).astype(jnp.bfloat16)
        part = jnp.dot(h, wout_ref[lo:lo + FFN_CHUNK, :], preferred_element_type=jnp.float32)
        if c == 0:
            acc_ref[...] = part
        else:
            acc_ref[...] += part
    o_ref[...] = _layer_norm(ALPHA * x1 + acc_ref[...], g2_ref[...], b2_ref[...])


def _tail(a_out, b_out, x2d, wo, g1, b1, win, wout, g2, b2):
    m = x2d.shape[0]
    tm = TAIL_ROWS
    assert D_FF % FFN_CHUNK == 0 and m % tm == 0
    row_tile = lambda w: pl.BlockSpec((tm, w), lambda i: (i, 0))
    whole = lambda a: pl.BlockSpec(a.shape, lambda i: (0, 0), pipeline_mode=pl.Buffered(1))
    return pl.pallas_call(
        _tail_kernel,
        grid=(m // tm,),
        in_specs=[row_tile(DIFF_WIDTH), row_tile(MOBA_WIDTH), row_tile(D_MODEL),
                  whole(wo), whole(g1), whole(b1), whole(win), whole(wout), whole(g2), whole(b2)],
        out_specs=row_tile(D_MODEL),
        out_shape=jax.ShapeDtypeStruct(x2d.shape, jnp.float32),
        scratch_shapes=[pltpu.VMEM((tm, D_MODEL), jnp.float32)],
        compiler_params=pltpu.CompilerParams(dimension_semantics=("arbitrary",),
                                             vmem_limit_bytes=VMEM_LIMIT),
        name="tail",
    )(a_out, b_out, x2d, wo, g1, b1, win, wout, g2, b2)


def _split_w_in(w):
    dk0, dv0, mq0 = DIFF_WIDTH, 2 * DIFF_WIDTH, 3 * DIFF_WIDTH
    mk0, mv0 = mq0 + MOBA_WIDTH, mq0 + 2 * MOBA_WIDTH
    w_k = jnp.concatenate([w[:, dk0:dv0], w[:, mk0:mv0]], axis=1)
    w_t = jnp.concatenate([w[:, :dk0], w[:, mq0:mk0], w[:, dv0:mq0], w[:, mv0:]], axis=1).T
    return w_k.astype(jnp.bfloat16), w_t.astype(jnp.bfloat16)


def kernel(x, w_in, diff_lambda, diff_subln_g, w_o, ln1_g, ln1_b, w_ffn_in, w_ffn_out, ln2_g, ln2_b):
    batch, seq, d = x.shape
    assert d == D_MODEL and seq % ATTN_TILE == 0 and w_in.shape[0] == DEPTH
    x2d = x.reshape(batch * seq, d)
    for l in range(DEPTH):
        w_k, w_t = _split_w_in(w_in[l])
        dk, mk, dqt, mqt, dvt, mvt = _proj(x2d, w_k, w_t, seq)
        a_out = _diff_attn(dqt, dk, dvt, diff_lambda[l], diff_subln_g[l][None, :], batch, seq)
        b_out = _moba_attn(mqt, mk, mvt, batch, seq)
        x2d = _tail(a_out, b_out, x2d, w_o[l].astype(jnp.bfloat16),
                    ln1_g[l][None, :], ln1_b[l][None, :],
                    w_ffn_in[l].astype(jnp.bfloat16), w_ffn_out[l].astype(jnp.bfloat16),
                    ln2_g[l][None, :], ln2_b[l][None, :])
    return x2d.reshape(batch, seq, d)
```

```python
import functools
import math

import numpy as np
import jax
import jax.numpy as jnp
from jax import lax
from jax.experimental import pallas as pl
from jax.experimental.pallas import tpu as pltpu

D_MODEL = 1024
HEAD_DIM = 64
N_DIFF_HEADS = 4
DIFF_WIDTH = N_DIFF_HEADS * 2 * HEAD_DIM
N_MOBA_HEADS = 8
MOBA_WIDTH = N_MOBA_HEADS * HEAD_DIM
ROT_DIM = HEAD_DIM // 4
ROT_HALF = ROT_DIM // 2
ROPE_THETA = 500000.0
MOBA_BLOCK = 256
MOBA_TOPK = 3
D_FF = 2816
DEPTH = 1
ALPHA = (2.0 * DEPTH) ** 0.25
LN_EPS = 1e-5
LAMBDA_INIT = 0.8 - 0.6 * math.exp(-0.3 * 0)
Q_SCALE = HEAD_DIM ** -0.5 * math.log2(math.e)

V7X_LANES = 128
V7X_SUBLANES = 8
V7X_BF16_ROWS = 16
V7X_VMEM_BYTES = 64 * 1024 * 1024
VMEM_LIMIT = V7X_VMEM_BYTES - 8 * 1024 * 1024

ATTN_TILE = 512
PROJ_ROWS = ATTN_TILE
TAIL_ROWS = 512
FFN_CHUNK = 256
ONES_ROWS = V7X_BF16_ROWS
STEP_UNROLL = 2
NEG_BIG = -1e30

_NT = (((1,), (1,)), ((), ()))


def _rope_tables(seq):
    inv = ROPE_THETA ** (-np.arange(0, ROT_DIM, 2, dtype=np.float64) / ROT_DIM)
    ang = np.arange(seq, dtype=np.float64)[:, None] * inv[None, :]
    cos = np.ones((seq, V7X_LANES), np.float64)
    sin = np.zeros((seq, V7X_LANES), np.float64)
    hot = np.zeros((seq, V7X_LANES), np.float64)
    for base in (0, HEAD_DIM):
        cos[:, base:base + ROT_HALF] = np.cos(ang)
        cos[:, base + ROT_HALF:base + ROT_DIM] = np.cos(ang)
        sin[:, base:base + ROT_HALF] = -np.sin(ang)
        sin[:, base + ROT_HALF:base + ROT_DIM] = np.sin(ang)
    blk = np.arange(seq) // MOBA_BLOCK
    hot[np.arange(seq), blk] = 1.0
    hot[np.arange(seq), HEAD_DIM + blk] = 1.0
    f32 = lambda a: jnp.asarray(a, jnp.float32)
    return f32(cos), f32(sin), f32(hot), f32(np.cos(ang).T), f32(np.sin(ang).T)


def _proj_kernel(x_ref, wk_ref, wt_ref, cos_ref, sin_ref, hot_ref, cost_ref, sint_ref,
                 dk_ref, mk_ref, dqt_ref, mqt_ref, dvt_ref, mvt_ref, kmean_ref,
                 *, tiles_per_seq):
    tm = x_ref.shape[0]
    t = pl.program_id(0) % tiles_per_seq
    xb = x_ref[...].astype(jnp.bfloat16)

    cos = cos_ref[...]
    sin = sin_ref[...]
    lane = lax.broadcasted_iota(jnp.int32, (tm, V7X_LANES), 1)
    first_half = (lane & (HEAD_DIM - 1)) < ROT_HALF

    def rope_rows(a):
        out = []
        for c in range(a.shape[1] // V7X_LANES):
            blk = a[:, c * V7X_LANES:(c + 1) * V7X_LANES]
            partner = jnp.where(first_half,
                                pltpu.roll(blk, V7X_LANES - ROT_HALF, 1),
                                pltpu.roll(blk, ROT_HALF, 1))
            out.append(blk * cos + partner * sin)
        return out

    dk = jnp.dot(xb, wk_ref[:, :DIFF_WIDTH], preferred_element_type=jnp.float32)
    for c, blk in enumerate(rope_rows(dk)):
        dk_ref[:, c * V7X_LANES:(c + 1) * V7X_LANES] = blk.astype(jnp.bfloat16)
    kblocks = rope_rows(jnp.dot(xb, wk_ref[:, DIFF_WIDTH:], preferred_element_type=jnp.float32))

    @pl.when(t == 0)
    def _():
        kmean_ref[...] = jnp.zeros_like(kmean_ref)

    blocks_per_tile = tm // MOBA_BLOCK
    n_blocks = kmean_ref.shape[0]
    km_row = lax.broadcasted_iota(jnp.int32, (n_blocks, V7X_LANES), 0)
    for c, blk in enumerate(kblocks):
        km_c = kmean_ref[:, c * V7X_LANES:(c + 1) * V7X_LANES]
        for r in range(blocks_per_tile):
            mean = jnp.sum(blk[r * MOBA_BLOCK:(r + 1) * MOBA_BLOCK], axis=0, keepdims=True) \
                * (1.0 / MOBA_BLOCK)
            km_c = jnp.where(km_row == blocks_per_tile * t + r, mean, km_c)
        kmean_ref[:, c * V7X_LANES:(c + 1) * V7X_LANES] = km_c

    hot = hot_ref[...]
    for h in range(N_MOBA_HEADS):
        own_lanes = (lane < HEAD_DIM) if h % 2 == 0 else (lane >= HEAD_DIM)
        k_aug = jnp.where(own_lanes, kblocks[h // 2], hot)
        mk_ref[:, h * V7X_LANES:(h + 1) * V7X_LANES] = k_aug.astype(jnp.bfloat16)

    cost = cost_ref[...]
    sint = sint_ref[...]

    def seg_t(j):
        return lax.dot_general(wt_ref[j * DIFF_WIDTH:(j + 1) * DIFF_WIDTH, :], xb, _NT,
                               preferred_element_type=jnp.float32)

    def rope_t(a):
        x1 = a[:ROT_HALF]
        x2 = a[ROT_HALF:ROT_DIM]
        return jnp.concatenate([x1 * cost - x2 * sint, x2 * cost + x1 * sint, a[ROT_DIM:]],
                               axis=0) * Q_SCALE

    dqt = seg_t(0)
    for h in range(N_DIFF_HEADS):
        maps = [rope_t(dqt[(2 * h + c) * HEAD_DIM:(2 * h + c + 1) * HEAD_DIM]) for c in range(2)]
        dqt_ref[0, h] = jnp.concatenate(maps, axis=0).astype(jnp.bfloat16)
    mqt = seg_t(1)
    q_heads = [rope_t(mqt[h * HEAD_DIM:(h + 1) * HEAD_DIM]) for h in range(N_MOBA_HEADS)]
    dvt = seg_t(2)
    for h in range(dvt_ref.shape[1]):
        dvt_ref[0, h] = dvt[h * V7X_LANES:(h + 1) * V7X_LANES].astype(jnp.bfloat16)
    mvt = seg_t(3)
    for g in range(mvt_ref.shape[1]):
        mvt_ref[0, g] = mvt[g * V7X_LANES:(g + 1) * V7X_LANES].astype(jnp.bfloat16)

    km = kmean_ref[...]
    km_rows = jnp.concatenate([km] * N_MOBA_HEADS, axis=0)
    r_id = lax.broadcasted_iota(jnp.int32, km_rows.shape, 0)
    l_id = lax.broadcasted_iota(jnp.int32, km_rows.shape, 1)
    km_rows = jnp.where((r_id // n_blocks) == (l_id // HEAD_DIM), km_rows, 0.0)
    gate_t = jnp.dot(km_rows, jnp.concatenate(q_heads, axis=0), precision=lax.Precision.HIGHEST,
                     preferred_element_type=jnp.float32)

    n_id = lax.broadcasted_iota(jnp.int32, (n_blocks, tm), 0)
    pos = lax.broadcasted_iota(jnp.int32, (n_blocks, tm), 1)
    own = blocks_per_tile * t + pos // MOBA_BLOCK
    past = n_id < own
    zeros = jnp.zeros((HEAD_DIM - n_blocks, tm), jnp.float32)
    for h in range(N_MOBA_HEADS):
        gm = jnp.where(past, gate_t[h * n_blocks:(h + 1) * n_blocks], -jnp.inf)
        rank = jnp.zeros((n_blocks, tm), jnp.int32)
        for m in range(n_blocks):
            g_m = gm[m:m + 1]
            rank = rank + jnp.where(n_id > m, jnp.where(g_m >= gm, 1, 0),
                                    jnp.where(g_m > gm, 1, 0))
        bias = jnp.where(n_id == own, 0.0,
                         jnp.where(past & (rank < MOBA_TOPK), 0.0, NEG_BIG))
        parts = [q_heads[h], bias, zeros] if h % 2 == 0 else [bias, zeros, q_heads[h]]
        mqt_ref[0, h] = jnp.concatenate(parts, axis=0).astype(jnp.bfloat16)


def _proj(x2d, w_k, w_t, seq):
    m = x2d.shape[0]
    tm = PROJ_ROWS
    tiles_per_seq = seq // tm
    n_blocks = seq // MOBA_BLOCK
    assert n_blocks * N_MOBA_HEADS == V7X_LANES and n_blocks <= HEAD_DIM
    cos, sin, hot, cost, sint = _rope_tables(seq)
    row_tile = lambda w: pl.BlockSpec((tm, w), lambda i: (i, 0))
    whole = lambda a: pl.BlockSpec(a.shape, lambda i: (0, 0))
    table = pl.BlockSpec((tm, V7X_LANES), lambda i: (i % tiles_per_seq, 0))
    table_t = pl.BlockSpec((ROT_HALF, tm), lambda i: (0, i % tiles_per_seq))
    t_tile = lambda n: pl.BlockSpec((1, n, V7X_LANES, tm), lambda i: (i, 0, 0, 0))
    bf = lambda w: jax.ShapeDtypeStruct((m, w), jnp.bfloat16)
    t_shape = lambda n: jax.ShapeDtypeStruct((m // tm, n, V7X_LANES, tm), jnp.bfloat16)
    n_dv = DIFF_WIDTH // V7X_LANES
    n_mv = MOBA_WIDTH // V7X_LANES
    return pl.pallas_call(
        functools.partial(_proj_kernel, tiles_per_seq=tiles_per_seq),
        grid=(m // tm,),
        in_specs=[row_tile(D_MODEL), whole(w_k), whole(w_t), table, table, table,
                  table_t, table_t],
        out_specs=[row_tile(DIFF_WIDTH), row_tile(2 * MOBA_WIDTH),
                   t_tile(N_DIFF_HEADS), t_tile(N_MOBA_HEADS), t_tile(n_dv), t_tile(n_mv)],
        out_shape=[bf(DIFF_WIDTH), bf(2 * MOBA_WIDTH),
                   t_shape(N_DIFF_HEADS), t_shape(N_MOBA_HEADS), t_shape(n_dv), t_shape(n_mv)],
        scratch_shapes=[pltpu.VMEM((n_blocks, MOBA_WIDTH), jnp.float32)],
        compiler_params=pltpu.CompilerParams(dimension_semantics=("arbitrary",),
                                             vmem_limit_bytes=VMEM_LIMIT),
        name="proj",
    )(x2d, w_k, w_t, cos, sin, hot, cost, sint)


def _causal_mask_t(t):
    key = lax.broadcasted_iota(jnp.int32, (t, t), 0)
    qry = lax.broadcasted_iota(jnp.int32, (t, t), 1)
    return key <= qry


def _with_ones(vt):
    return jnp.concatenate([vt, jnp.ones((ONES_ROWS, vt.shape[1]), vt.dtype)], axis=0)


def _step_table(nq):
    pairs = [(i, i) for i in range(nq)]
    pairs += [(i, j) for j in range(nq - 1) for i in range(j + 1, nq)]
    pairs.append(pairs[-1])
    return jnp.asarray(np.array(pairs, np.int32).T)


def _flash_all_tiles(tab_ref, nq, t, scores, pv, s_ref, p_ref, m_ref, acc_ref, smax_ref):
    n_steps = nq * (nq + 1) // 2
    assert nq % STEP_UNROLL == 0 and (n_steps - nq) % STEP_UNROLL == 0 and STEP_UNROLL % 2 == 0
    mask = _causal_mask_t(t)

    def put_scores(k, slot, masked):
        s = scores(tab_ref[0, k], tab_ref[1, k])
        if masked is True:
            s = [jnp.where(mask, x, -jnp.inf) for x in s]
        elif masked is not False:
            hide = jnp.where(mask, 0.0, jnp.where(masked, -jnp.inf, 0.0))
            s = [x + hide for x in s]
        for x in range(2):
            s_ref[slot, x] = s[x]
            smax_ref[slot, x, 0:1, :] = jnp.max(s[x], axis=0, keepdims=True)

    def softmax(k, slot, first_tile):
        i = tab_ref[0, k]
        alpha = []
        for x in range(2):
            m_new = smax_ref[slot, x, 0:1, :]
            if first_tile:
                alpha.append(jnp.zeros((1, t), jnp.float32))
            else:
                m_old = m_ref[i, x, 0:1, :]
                m_new = jnp.maximum(m_old, m_new)
                alpha.append(jnp.exp2(m_old - m_new))
            p_ref[slot, x] = jnp.exp2(s_ref[slot, x] - m_new).astype(jnp.bfloat16)
            m_ref[i, x, 0:1, :] = m_new
        return alpha

    def add_values(k, slot, alpha, first_tile):
        i = tab_ref[0, k]
        o = pv(tab_ref[1, k], [p_ref[slot, 0], p_ref[slot, 1]])
        for x in range(2):
            acc_ref[i, x] = o[x] if first_tile else alpha[x] * acc_ref[i, x] + o[x]

    def trip(first, first_tiles, next_masked):
        def body(u, a_prev):
            k0 = first + STEP_UNROLL * u
            for r in range(STEP_UNROLL):
                k = k0 + r
                put_scores(k + 1, (r + 1) % 2, next_masked(k + 1, r))
                a_cur = softmax(k, r % 2, first_tiles)
                add_values(jnp.maximum(k - 1, 0), (r + 1) % 2, a_prev, first_tiles)
                a_prev = a_cur
            return a_prev
        return body

    acc_ref[nq - 1] = jnp.zeros(acc_ref.shape[1:], acc_ref.dtype)
    p_ref[1] = jnp.zeros(p_ref.shape[1:], p_ref.dtype)
    put_scores(0, 0, True)
    zeros = [jnp.zeros((1, t), jnp.float32)] * 2
    diag_next = lambda k, r: True if r < STEP_UNROLL - 1 else k < nq
    a_prev = lax.fori_loop(0, nq // STEP_UNROLL, trip(0, True, diag_next), zeros)
    a_prev = lax.fori_loop(0, (n_steps - nq) // STEP_UNROLL,
                           trip(nq, False, lambda k, r: False), a_prev)
    add_values(n_steps - 1, (n_steps - 1) % 2, a_prev, False)


def _flash_scratch(nq, t, value_rows):
    return [pltpu.VMEM((2, 2, t, t), jnp.float32), pltpu.VMEM((2, 2, t, t), jnp.bfloat16),
            pltpu.VMEM((nq, 2, V7X_SUBLANES, t), jnp.float32),
            pltpu.VMEM((nq, 2, value_rows + ONES_ROWS, t), jnp.float32),
            pltpu.VMEM((2, 2, V7X_SUBLANES, t), jnp.float32)]


def _flash_grid_spec(grid, in_specs, out_specs, nq, t, value_rows):
    return pltpu.PrefetchScalarGridSpec(
        num_scalar_prefetch=1, grid=grid, in_specs=in_specs, out_specs=out_specs,
        scratch_shapes=_flash_scratch(nq, t, value_rows))


def _diff_kernel(tab_ref, lam_ref, g_ref, qt_ref, k_ref, vt_ref, o_ref,
                 s_ref, p_ref, m_ref, acc_ref, smax_ref):
    nq, _, width, t = qt_ref.shape
    feat = lax.broadcasted_iota(jnp.int32, (width, t), 0)

    def scores(i, j):
        qt = qt_ref[i, 0]
        zero = jnp.zeros_like(qt)
        k = k_ref[pl.ds(pl.multiple_of(j * t, t), t), :]
        return [jnp.dot(k, jnp.where(feat < HEAD_DIM, qt, zero),
                        preferred_element_type=jnp.float32),
                jnp.dot(k, jnp.where(feat >= HEAD_DIM, qt, zero),
                        preferred_element_type=jnp.float32)]

    def pv(j, p):
        lhs = _with_ones(vt_ref[j, 0])
        return [jnp.dot(lhs, p[x], preferred_element_type=jnp.float32) for x in range(2)]

    _flash_all_tiles(tab_ref, nq, t, scores, pv, s_ref, p_ref, m_ref, acc_ref, smax_ref)

    lp = lam_ref[...]
    lam = (jnp.exp(jnp.sum(lp[0:1] * lp[1:2], axis=1, keepdims=True))
           - jnp.exp(jnp.sum(lp[2:3] * lp[3:4], axis=1, keepdims=True)) + LAMBDA_INIT)
    gain = g_ref[...] * (1.0 - LAMBDA_INIT)

    def finish(i, carry):
        acc1 = acc_ref[i, 0]
        acc2 = acc_ref[i, 1]
        o = (acc1[:width] * (1.0 / acc1[width:width + 1])
             - lam * (acc2[:width] * (1.0 / acc2[width:width + 1])))
        ms = jnp.mean(o * o, axis=0, keepdims=True)
        o = (o * lax.rsqrt(ms + LN_EPS)).T * gain
        o_ref[pl.ds(pl.multiple_of(i * t, t), t), :] = o.astype(o_ref.dtype)
        return carry

    lax.fori_loop(0, nq, finish, 0)


def _diff_attn(dqt, dk, dvt, lam_params, subln_g, batch, seq):
    t = ATTN_TILE
    nq = seq // t
    width = 2 * HEAD_DIM
    tab = _step_table(nq)
    tile_spec = pl.BlockSpec((nq, 1, width, t), lambda b, h, tab: (b, h, 0, 0))
    row_spec = pl.BlockSpec((seq, width), lambda b, h, tab: (b, h))
    whole = lambda a: pl.BlockSpec(a.shape, lambda b, h, tab: (0, 0))
    return pl.pallas_call(
        _diff_kernel,
        grid_spec=_flash_grid_spec(
            (batch, N_DIFF_HEADS),
            [whole(lam_params), whole(subln_g), tile_spec, row_spec, tile_spec],
            row_spec, nq, t, width),
        out_shape=jax.ShapeDtypeStruct(dk.shape, jnp.bfloat16),
        compiler_params=pltpu.CompilerParams(
            dimension_semantics=("arbitrary", "arbitrary"), vmem_limit_bytes=VMEM_LIMIT),
        name="diff_attn",
    )(tab, lam_params, subln_g, dqt, dk, dvt)


def _moba_kernel(tab_ref, qt_ref, k_ref, vt_ref, o_ref, s_ref, p_ref, m_ref, acc_ref, smax_ref):
    nq, _, _, t = qt_ref.shape

    def scores(i, j):
        k = k_ref[pl.ds(pl.multiple_of(j * t, t), t), :]
        return [jnp.dot(k[:, x * V7X_LANES:(x + 1) * V7X_LANES], qt_ref[i, x],
                        preferred_element_type=jnp.float32) for x in range(2)]

    def pv(j, p):
        vt = vt_ref[j, 0]
        return [jnp.dot(_with_ones(vt[x * HEAD_DIM:(x + 1) * HEAD_DIM]), p[x],
                        preferred_element_type=jnp.float32) for x in range(2)]

    _flash_all_tiles(tab_ref, nq, t, scores, pv, s_ref, p_ref, m_ref, acc_ref, smax_ref)

    def finish(i, carry):
        acca = acc_ref[i, 0]
        accb = acc_ref[i, 1]
        o = jnp.concatenate([acca[:HEAD_DIM] * (1.0 / acca[HEAD_DIM:HEAD_DIM + 1]),
                             accb[:HEAD_DIM] * (1.0 / accb[HEAD_DIM:HEAD_DIM + 1])],
                            axis=0)
        o_ref[pl.ds(pl.multiple_of(i * t, t), t), :] = o.T.astype(o_ref.dtype)
        return carry

    lax.fori_loop(0, nq, finish, 0)


def _moba_attn(mqt, mk, mvt, batch, seq):
    t = ATTN_TILE
    nq = seq // t
    pairs = N_MOBA_HEADS // 2
    tab = _step_table(nq)
    return pl.pallas_call(
        _moba_kernel,
        grid_spec=_flash_grid_spec(
            (batch, pairs),
            [pl.BlockSpec((nq, 2, V7X_LANES, t), lambda b, g, tab: (b, g, 0, 0)),
             pl.BlockSpec((seq, 2 * V7X_LANES), lambda b, g, tab: (b, g)),
             pl.BlockSpec((nq, 1, V7X_LANES, t), lambda b, g, tab: (b, g, 0, 0))],
            pl.BlockSpec((seq, V7X_LANES), lambda b, g, tab: (b, g)), nq, t, HEAD_DIM),
        out_shape=jax.ShapeDtypeStruct((batch * seq, MOBA_WIDTH), jnp.bfloat16),
        compiler_params=pltpu.CompilerParams(
            dimension_semantics=("arbitrary", "arbitrary"), vmem_limit_bytes=VMEM_LIMIT),
        name="moba_attn",
    )(tab, mqt, mk, mvt)


def _layer_norm(y, g, b):
    mu = jnp.mean(y, axis=-1, keepdims=True)
    d = y - mu
    var = jnp.mean(d * d, axis=-1, keepdims=True)
    return d * lax.rsqrt(var + LN_EPS) * g + b


def _tail_kernel(a_ref, b_ref, x_ref, wo_ref, g1_ref, b1_ref, win_ref, wout_ref, g2_ref, b2_ref,
                 o_ref, acc_ref):
    mix = jnp.dot(a_ref[...], wo_ref[:DIFF_WIDTH, :], preferred_element_type=jnp.float32)
    mix = mix + jnp.dot(b_ref[...], wo_ref[DIFF_WIDTH:, :], preferred_element_type=jnp.float32)
    x1 = _layer_norm(ALPHA * x_ref[...] + mix, g1_ref[...], b1_ref[...])
    xb = x1.astype(jnp.bfloat16)
    for c in range(D_FF // FFN_CHUNK):
        lo = c * FFN_CHUNK
        gate = jnp.dot(xb, win_ref[:, lo:lo + FFN_CHUNK], preferred_element_type=jnp.float32)
        up = jnp.dot(xb, win_ref[:, D_FF + lo:D_FF + lo + FFN_CHUNK],
                     preferred_element_type=jnp.float32)
        h = (gate * jax.nn.sigmoid(gate) * up).astype(jnp.bfloat16)
        part = jnp.dot(h, wout_ref[lo:lo + FFN_CHUNK, :], preferred_element_type=jnp.float32)
        if c == 0:
            acc_ref[...] = part
        else:
            acc_ref[...] += part
    o_ref[...] = _layer_norm(ALPHA * x1 + acc_ref[...], g2_ref[...], b2_ref[...])


def _tail(a_out, b_out, x2d, wo, g1, b1, win, wout, g2, b2):
    m = x2d.shape[0]
    tm = TAIL_ROWS
    assert D_FF % FFN_CHUNK == 0 and m % tm == 0
    row_tile = lambda w: pl.BlockSpec((tm, w), lambda i: (i, 0))
    whole = lambda a: pl.BlockSpec(a.shape, lambda i: (0, 0), pipeline_mode=pl.Buffered(1))
    return pl.pallas_call(
        _tail_kernel,
        grid=(m // tm,),
        in_specs=[row_tile(DIFF_WIDTH), row_tile(MOBA_WIDTH), row_tile(D_MODEL),
                  whole(wo), whole(g1), whole(b1), whole(win), whole(wout), whole(g2), whole(b2)],
        out_specs=row_tile(D_MODEL),
        out_shape=jax.ShapeDtypeStruct(x2d.shape, jnp.float32),
        scratch_shapes=[pltpu.VMEM((tm, D_MODEL), jnp.float32)],
        compiler_params=pltpu.CompilerParams(dimension_semantics=("arbitrary",),
                                             vmem_limit_bytes=VMEM_LIMIT),
        name="tail",
    )(a_out, b_out, x2d, wo, g1, b1, win, wout, g2, b2)


def _split_w_in(w):
    dk0, dv0, mq0 = DIFF_WIDTH, 2 * DIFF_WIDTH, 3 * DIFF_WIDTH
    mk0, mv0 = mq0 + MOBA_WIDTH, mq0 + 2 * MOBA_WIDTH
    w_k = jnp.concatenate([w[:, dk0:dv0], w[:, mk0:mv0]], axis=1)
    w_t = jnp.concatenate([w[:, :dk0], w[:, mq0:mk0], w[:, dv0:mq0], w[:, mv0:]], axis=1).T
    return w_k.astype(jnp.bfloat16), w_t.astype(jnp.bfloat16)


def kernel(x, w_in, diff_lambda, diff_subln_g, w_o, ln1_g, ln1_b, w_ffn_in, w_ffn_out, ln2_g, ln2_b):
    batch, seq, d = x.shape
    assert d == D_MODEL and seq % ATTN_TILE == 0 and w_in.shape[0] == DEPTH
    x2d = x.reshape(batch * seq, d)
    for l in range(DEPTH):
        w_k, w_t = _split_w_in(w_in[l])
        dk, mk, dqt, mqt, dvt, mvt = _proj(x2d, w_k, w_t, seq)
        a_out = _diff_attn(dqt, dk, dvt, diff_lambda[l], diff_subln_g[l][None, :], batch, seq)
        b_out = _moba_attn(mqt, mk, mvt, batch, seq)
        x2d = _tail(a_out, b_out, x2d, w_o[l].astype(jnp.bfloat16),
                    ln1_g[l][None, :], ln1_b[l][None, :],
                    w_ffn_in[l].astype(jnp.bfloat16), w_ffn_out[l].astype(jnp.bfloat16),
                    ln2_g[l][None, :], ln2_b[l][None, :])
    return x2d.reshape(batch, seq, d)
```

```python
import functools
import math

import numpy as np
import jax
import jax.numpy as jnp
from jax import lax
from jax.experimental import pallas as pl
from jax.experimental.pallas import tpu as pltpu

D_MODEL = 1024
HEAD_DIM = 64
N_DIFF_HEADS = 4
DIFF_WIDTH = N_DIFF_HEADS * 2 * HEAD_DIM
N_MOBA_HEADS = 8
MOBA_WIDTH = N_MOBA_HEADS * HEAD_DIM
ROT_DIM = HEAD_DIM // 4
ROT_HALF = ROT_DIM // 2
ROPE_THETA = 500000.0
MOBA_BLOCK = 256
MOBA_TOPK = 3
D_FF = 2816
DEPTH = 1
ALPHA = (2.0 * DEPTH) ** 0.25
LN_EPS = 1e-5
LAMBDA_INIT = 0.8 - 0.6 * math.exp(-0.3 * 0)
Q_SCALE = HEAD_DIM ** -0.5 * math.log2(math.e)

V7X_LANES = 128
V7X_SUBLANES = 8
V7X_BF16_ROWS = 16
V7X_VMEM_BYTES = 64 * 1024 * 1024
VMEM_LIMIT = V7X_VMEM_BYTES - 8 * 1024 * 1024

ATTN_TILE = 512
PROJ_ROWS = ATTN_TILE
TAIL_ROWS = 512
FFN_CHUNK = 256
ONES_ROWS = V7X_BF16_ROWS
STEP_UNROLL = 2
NEG_BIG = -1e30

_NT = (((1,), (1,)), ((), ()))


def _rope_tables(seq):
    inv = ROPE_THETA ** (-np.arange(0, ROT_DIM, 2, dtype=np.float64) / ROT_DIM)
    ang = np.arange(seq, dtype=np.float64)[:, None] * inv[None, :]
    cos = np.ones((seq, V7X_LANES), np.float64)
    sin = np.zeros((seq, V7X_LANES), np.float64)
    hot = np.zeros((seq, V7X_LANES), np.float64)
    for base in (0, HEAD_DIM):
        cos[:, base:base + ROT_HALF] = np.cos(ang)
        cos[:, base + ROT_HALF:base + ROT_DIM] = np.cos(ang)
        sin[:, base:base + ROT_HALF] = -np.sin(ang)
        sin[:, base + ROT_HALF:base + ROT_DIM] = np.sin(ang)
    blk = np.arange(seq) // MOBA_BLOCK
    hot[np.arange(seq), blk] = 1.0
    hot[np.arange(seq), HEAD_DIM + blk] = 1.0
    f32 = lambda a: jnp.asarray(a, jnp.float32)
    return f32(cos), f32(sin), f32(hot), f32(np.cos(ang).T), f32(np.sin(ang).T)


def _split_bf16(a):
    hi = a.astype(jnp.bfloat16)
    return hi, (a - hi.astype(jnp.float32)).astype(jnp.bfloat16)


def _proj_kernel(x_ref, wk_ref, wt_ref, cos_ref, sin_ref, hot_ref, cost_ref, sint_ref,
                 dk_ref, mk_ref, dqt_ref, mqt_ref, dvt_ref, mvt_ref, kmean_ref,
                 *, tiles_per_seq):
    tm = x_ref.shape[0]
    t = pl.program_id(0) % tiles_per_seq

    @pl.when(t == 0)
    def _():
        kmean_ref[...] = jnp.zeros_like(kmean_ref)

    xb = x_ref[...].astype(jnp.bfloat16)

    cos = cos_ref[...]
    sin = sin_ref[...]
    lane = lax.broadcasted_iota(jnp.int32, (tm, V7X_LANES), 1)
    first_half = (lane & (HEAD_DIM - 1)) < ROT_HALF

    def rope_rows(a):
        out = []
        for c in range(a.shape[1] // V7X_LANES):
            blk = a[:, c * V7X_LANES:(c + 1) * V7X_LANES]
            partner = jnp.where(first_half,
                                pltpu.roll(blk, V7X_LANES - ROT_HALF, 1),
                                pltpu.roll(blk, ROT_HALF, 1))
            out.append(blk * cos + partner * sin)
        return out

    dk = jnp.dot(xb, wk_ref[:, :DIFF_WIDTH], preferred_element_type=jnp.float32)
    for c, blk in enumerate(rope_rows(dk)):
        dk_ref[:, c * V7X_LANES:(c + 1) * V7X_LANES] = blk.astype(jnp.bfloat16)
    kblocks = rope_rows(jnp.dot(xb, wk_ref[:, DIFF_WIDTH:], preferred_element_type=jnp.float32))

    blocks_per_tile = tm // MOBA_BLOCK
    n_blocks = kmean_ref.shape[0]
    km_row = lax.broadcasted_iota(jnp.int32, (n_blocks, V7X_LANES), 0)
    for c, blk in enumerate(kblocks):
        km_c = kmean_ref[:, c * V7X_LANES:(c + 1) * V7X_LANES]
        for r in range(blocks_per_tile):
            mean = jnp.sum(blk[r * MOBA_BLOCK:(r + 1) * MOBA_BLOCK], axis=0, keepdims=True) \
                * (1.0 / MOBA_BLOCK)
            km_c = jnp.where(km_row == blocks_per_tile * t + r, mean, km_c)
        kmean_ref[:, c * V7X_LANES:(c + 1) * V7X_LANES] = km_c

    hot = hot_ref[...]
    for h in range(N_MOBA_HEADS):
        own_lanes = (lane < HEAD_DIM) if h % 2 == 0 else (lane >= HEAD_DIM)
        k_aug = jnp.where(own_lanes, kblocks[h // 2], hot)
        mk_ref[:, h * V7X_LANES:(h + 1) * V7X_LANES] = k_aug.astype(jnp.bfloat16)

    cost = cost_ref[...]
    sint = sint_ref[...]

    def seg_t(j):
        return lax.dot_general(wt_ref[j * DIFF_WIDTH:(j + 1) * DIFF_WIDTH, :], xb, _NT,
                               preferred_element_type=jnp.float32)

    def rope_t(a):
        x1 = a[:ROT_HALF]
        x2 = a[ROT_HALF:ROT_DIM]
        return jnp.concatenate([x1 * cost - x2 * sint, x2 * cost + x1 * sint, a[ROT_DIM:]],
                               axis=0) * Q_SCALE

    mqt = seg_t(1)
    q_heads = [rope_t(mqt[h * HEAD_DIM:(h + 1) * HEAD_DIM]) for h in range(N_MOBA_HEADS)]

    km = kmean_ref[...]
    km_rows = jnp.concatenate([km] * N_MOBA_HEADS, axis=0)
    r_id = lax.broadcasted_iota(jnp.int32, km_rows.shape, 0)
    l_id = lax.broadcasted_iota(jnp.int32, km_rows.shape, 1)
    km_rows = jnp.where((r_id // n_blocks) == (l_id // HEAD_DIM), km_rows, 0.0)
    k_hi, k_lo = _split_bf16(km_rows)
    q_hi, q_lo = _split_bf16(jnp.concatenate(q_heads, axis=0))
    mm = functools.partial(jnp.dot, preferred_element_type=jnp.float32)
    gate_t = mm(k_hi, q_hi) + mm(k_hi, q_lo) + mm(k_lo, q_hi)

    n_id = lax.broadcasted_iota(jnp.int32, (n_blocks, tm), 0)
    pos = lax.broadcasted_iota(jnp.int32, (n_blocks, tm), 1)
    own = blocks_per_tile * t + pos // MOBA_BLOCK
    past = n_id < own
    zeros = jnp.zeros((HEAD_DIM - n_blocks, tm), jnp.float32)
    for h in range(N_MOBA_HEADS):
        left = jnp.where(past, gate_t[h * n_blocks:(h + 1) * n_blocks], -jnp.inf)
        chosen = jnp.zeros((n_blocks, tm), jnp.int32)
        for _ in range(MOBA_TOPK):
            best = jnp.max(left, axis=0, keepdims=True)
            first = jnp.min(jnp.where(left == best, n_id, n_blocks), axis=0, keepdims=True)
            pick = n_id == first
            chosen = jnp.where(pick, 1, chosen)
            left = jnp.where(pick, -jnp.inf, left)
        bias = jnp.where(n_id == own, 0.0,
                         jnp.where(past & (chosen > 0), 0.0, NEG_BIG))
        parts = [q_heads[h], bias, zeros] if h % 2 == 0 else [bias, zeros, q_heads[h]]
        mqt_ref[0, h] = jnp.concatenate(parts, axis=0).astype(jnp.bfloat16)

    dqt = seg_t(0)
    for h in range(N_DIFF_HEADS):
        maps = [rope_t(dqt[(2 * h + c) * HEAD_DIM:(2 * h + c + 1) * HEAD_DIM]) for c in range(2)]
        dqt_ref[0, h] = jnp.concatenate(maps, axis=0).astype(jnp.bfloat16)
    dvt = seg_t(2)
    for h in range(dvt_ref.shape[1]):
        dvt_ref[0, h] = dvt[h * V7X_LANES:(h + 1) * V7X_LANES].astype(jnp.bfloat16)
    mvt = seg_t(3)
    for g in range(mvt_ref.shape[1]):
        mvt_ref[0, g] = mvt[g * V7X_LANES:(g + 1) * V7X_LANES].astype(jnp.bfloat16)


def _proj(x2d, w_k, w_t, seq):
    m = x2d.shape[0]
    tm = PROJ_ROWS
    tiles_per_seq = seq // tm
    n_blocks = seq // MOBA_BLOCK
    assert n_blocks * N_MOBA_HEADS == V7X_LANES and n_blocks <= HEAD_DIM
    cos, sin, hot, cost, sint = _rope_tables(seq)
    row_tile = lambda w: pl.BlockSpec((tm, w), lambda i: (i, 0))
    whole = lambda a: pl.BlockSpec(a.shape, lambda i: (0, 0))
    table = pl.BlockSpec((tm, V7X_LANES), lambda i: (i % tiles_per_seq, 0))
    table_t = pl.BlockSpec((ROT_HALF, tm), lambda i: (0, i % tiles_per_seq))
    t_tile = lambda n: pl.BlockSpec((1, n, V7X_LANES, tm), lambda i: (i, 0, 0, 0))
    bf = lambda w: jax.ShapeDtypeStruct((m, w), jnp.bfloat16)
    t_shape = lambda n: jax.ShapeDtypeStruct((m // tm, n, V7X_LANES, tm), jnp.bfloat16)
    n_dv = DIFF_WIDTH // V7X_LANES
    n_mv = MOBA_WIDTH // V7X_LANES
    return pl.pallas_call(
        functools.partial(_proj_kernel, tiles_per_seq=tiles_per_seq),
        grid=(m // tm,),
        in_specs=[row_tile(D_MODEL), whole(w_k), whole(w_t), table, table, table,
                  table_t, table_t],
        out_specs=[row_tile(DIFF_WIDTH), row_tile(2 * MOBA_WIDTH),
                   t_tile(N_DIFF_HEADS), t_tile(N_MOBA_HEADS), t_tile(n_dv), t_tile(n_mv)],
        out_shape=[bf(DIFF_WIDTH), bf(2 * MOBA_WIDTH),
                   t_shape(N_DIFF_HEADS), t_shape(N_MOBA_HEADS), t_shape(n_dv), t_shape(n_mv)],
        scratch_shapes=[pltpu.VMEM((n_blocks, MOBA_WIDTH), jnp.float32)],
        compiler_params=pltpu.CompilerParams(dimension_semantics=("arbitrary",),
                                             vmem_limit_bytes=VMEM_LIMIT),
        name="proj",
    )(x2d, w_k, w_t, cos, sin, hot, cost, sint)


def _causal_mask_t(t):
    key = lax.broadcasted_iota(jnp.int32, (t, t), 0)
    qry = lax.broadcasted_iota(jnp.int32, (t, t), 1)
    return key <= qry


def _with_ones(vt):
    return jnp.concatenate([vt, jnp.ones((ONES_ROWS, vt.shape[1]), vt.dtype)], axis=0)


def _step_table(nq):
    pairs = [(i, i) for i in range(nq)]
    pairs += [(i, j) for j in range(nq - 1) for i in range(j + 1, nq)]
    pairs.append(pairs[-1])
    return jnp.asarray(np.array(pairs, np.int32).T)


def _flash_all_tiles(tab_ref, nq, t, scores, pv, s_ref, p_ref, m_ref, acc_ref, smax_ref):
    n_steps = nq * (nq + 1) // 2
    assert nq % STEP_UNROLL == 0 and (n_steps - nq) % STEP_UNROLL == 0 and STEP_UNROLL % 2 == 0
    mask = _causal_mask_t(t)

    def put_scores(k, slot, masked):
        s = scores(tab_ref[0, k], tab_ref[1, k])
        if masked is True:
            s = [jnp.where(mask, x, -jnp.inf) for x in s]
        elif masked is not False:
            hide = jnp.where(mask, 0.0, jnp.where(masked, -jnp.inf, 0.0))
            s = [x + hide for x in s]
        for x in range(2):
            s_ref[slot, x] = s[x]
            smax_ref[slot, x, 0:1, :] = jnp.max(s[x], axis=0, keepdims=True)

    def softmax(k, slot, first_tile):
        i = tab_ref[0, k]
        alpha = []
        for x in range(2):
            m_new = smax_ref[slot, x, 0:1, :]
            if first_tile:
                alpha.append(jnp.zeros((1, t), jnp.float32))
            else:
                m_old = m_ref[i, x, 0:1, :]
                m_new = jnp.maximum(m_old, m_new)
                alpha.append(jnp.exp2(m_old - m_new))
            p_ref[slot, x] = jnp.exp2(s_ref[slot, x] - m_new).astype(jnp.bfloat16)
            m_ref[i, x, 0:1, :] = m_new
        return alpha

    def add_values(k, slot, alpha, first_tile):
        i = tab_ref[0, k]
        o = pv(tab_ref[1, k], [p_ref[slot, 0], p_ref[slot, 1]])
        for x in range(2):
            acc_ref[i, x] = o[x] if first_tile else alpha[x] * acc_ref[i, x] + o[x]

    def trip(first, first_tiles, next_masked):
        def body(u, a_prev):
            k0 = first + STEP_UNROLL * u
            for r in range(STEP_UNROLL):
                k = k0 + r
                put_scores(k + 1, (r + 1) % 2, next_masked(k + 1, r))
                a_cur = softmax(k, r % 2, first_tiles)
                add_values(jnp.maximum(k - 1, 0), (r + 1) % 2, a_prev, first_tiles)
                a_prev = a_cur
            return a_prev
        return body

    acc_ref[nq - 1] = jnp.zeros(acc_ref.shape[1:], acc_ref.dtype)
    p_ref[1] = jnp.zeros(p_ref.shape[1:], p_ref.dtype)
    put_scores(0, 0, True)
    zeros = [jnp.zeros((1, t), jnp.float32)] * 2
    diag_next = lambda k, r: True if r < STEP_UNROLL - 1 else k < nq
    a_prev = lax.fori_loop(0, nq // STEP_UNROLL, trip(0, True, diag_next), zeros)
    a_prev = lax.fori_loop(0, (n_steps - nq) // STEP_UNROLL,
                           trip(nq, False, lambda k, r: False), a_prev)
    add_values(n_steps - 1, (n_steps - 1) % 2, a_prev, False)


def _flash_scratch(nq, t, value_rows):
    return [pltpu.VMEM((2, 2, t, t), jnp.float32), pltpu.VMEM((2, 2, t, t), jnp.bfloat16),
            pltpu.VMEM((nq, 2, V7X_SUBLANES, t), jnp.float32),
            pltpu.VMEM((nq, 2, value_rows + ONES_ROWS, t), jnp.float32),
            pltpu.VMEM((2, 2, V7X_SUBLANES, t), jnp.float32)]


def _flash_grid_spec(grid, in_specs, out_specs, nq, t, value_rows):
    return pltpu.PrefetchScalarGridSpec(
        num_scalar_prefetch=1, grid=grid, in_specs=in_specs, out_specs=out_specs,
        scratch_shapes=_flash_scratch(nq, t, value_rows))


def _diff_kernel(tab_ref, lam_ref, g_ref, qt_ref, k_ref, vt_ref, o_ref,
                 s_ref, p_ref, m_ref, acc_ref, smax_ref):
    nq, _, width, t = qt_ref.shape
    feat = lax.broadcasted_iota(jnp.int32, (width, t), 0)

    def scores(i, j):
        qt = qt_ref[i, 0]
        zero = jnp.zeros_like(qt)
        k = k_ref[pl.ds(pl.multiple_of(j * t, t), t), :]
        return [jnp.dot(k, jnp.where(feat < HEAD_DIM, qt, zero),
                        preferred_element_type=jnp.float32),
                jnp.dot(k, jnp.where(feat >= HEAD_DIM, qt, zero),
                        preferred_element_type=jnp.float32)]

    def pv(j, p):
        lhs = _with_ones(vt_ref[j, 0])
        return [jnp.dot(lhs, p[x], preferred_element_type=jnp.float32) for x in range(2)]

    _flash_all_tiles(tab_ref, nq, t, scores, pv, s_ref, p_ref, m_ref, acc_ref, smax_ref)

    lp = lam_ref[...]
    lam = (jnp.exp(jnp.sum(lp[0:1] * lp[1:2], axis=1, keepdims=True))
           - jnp.exp(jnp.sum(lp[2:3] * lp[3:4], axis=1, keepdims=True)) + LAMBDA_INIT)
    gain = g_ref[...] * (1.0 - LAMBDA_INIT)

    def finish(i, carry):
        acc1 = acc_ref[i, 0]
        acc2 = acc_ref[i, 1]
        o = (acc1[:width] * (1.0 / acc1[width:width + 1])
             - lam * (acc2[:width] * (1.0 / acc2[width:width + 1])))
        ms = jnp.mean(o * o, axis=0, keepdims=True)
        o = (o * lax.rsqrt(ms + LN_EPS)).T * gain
        o_ref[pl.ds(pl.multiple_of(i * t, t), t), :] = o.astype(o_ref.dtype)
        return carry

    lax.fori_loop(0, nq, finish, 0)


def _diff_attn(dqt, dk, dvt, lam_params, subln_g, batch, seq):
    t = ATTN_TILE
    nq = seq // t
    width = 2 * HEAD_DIM
    tab = _step_table(nq)
    tile_spec = pl.BlockSpec((nq, 1, width, t), lambda b, h, tab: (b, h, 0, 0))
    row_spec = pl.BlockSpec((seq, width), lambda b, h, tab: (b, h))
    whole = lambda a: pl.BlockSpec(a.shape, lambda b, h, tab: (0, 0))
    return pl.pallas_call(
        _diff_kernel,
        grid_spec=_flash_grid_spec(
            (batch, N_DIFF_HEADS),
            [whole(lam_params), whole(subln_g), tile_spec, row_spec, tile_spec],
            row_spec, nq, t, width),
        out_shape=jax.ShapeDtypeStruct(dk.shape, jnp.bfloat16),
        compiler_params=pltpu.CompilerParams(
            dimension_semantics=("arbitrary", "arbitrary"), vmem_limit_bytes=VMEM_LIMIT),
        name="diff_attn",
    )(tab, lam_params, subln_g, dqt, dk, dvt)


def _moba_kernel(tab_ref, qt_ref, k_ref, vt_ref, o_ref, s_ref, p_ref, m_ref, acc_ref, smax_ref):
    nq, _, _, t = qt_ref.shape

    def scores(i, j):
        k = k_ref[pl.ds(pl.multiple_of(j * t, t), t), :]
        return [jnp.dot(k[:, x * V7X_LANES:(x + 1) * V7X_LANES], qt_ref[i, x],
                        preferred_element_type=jnp.float32) for x in range(2)]

    def pv(j, p):
        vt = vt_ref[j, 0]
        return [jnp.dot(_with_ones(vt[x * HEAD_DIM:(x + 1) * HEAD_DIM]), p[x],
                        preferred_element_type=jnp.float32) for x in range(2)]

    _flash_all_tiles(tab_ref, nq, t, scores, pv, s_ref, p_ref, m_ref, acc_ref, smax_ref)

    def finish(i, carry):
        acca = acc_ref[i, 0]
        accb = acc_ref[i, 1]
        o = jnp.concatenate([acca[:HEAD_DIM] * (1.0 / acca[HEAD_DIM:HEAD_DIM + 1]),
                             accb[:HEAD_DIM] * (1.0 / accb[HEAD_DIM:HEAD_DIM + 1])],
                            axis=0)
        o_ref[pl.ds(pl.multiple_of(i * t, t), t), :] = o.T.astype(o_ref.dtype)
        return carry

    lax.fori_loop(0, nq, finish, 0)


def _moba_attn(mqt, mk, mvt, batch, seq):
    t = ATTN_TILE
    nq = seq // t
    pairs = N_MOBA_HEADS // 2
    tab = _step_table(nq)
    return pl.pallas_call(
        _moba_kernel,
        grid_spec=_flash_grid_spec(
            (batch, pairs),
            [pl.BlockSpec((nq, 2, V7X_LANES, t), lambda b, g, tab: (b, g, 0, 0)),
             pl.BlockSpec((seq, 2 * V7X_LANES), lambda b, g, tab: (b, g)),
             pl.BlockSpec((nq, 1, V7X_LANES, t), lambda b, g, tab: (b, g, 0, 0))],
            pl.BlockSpec((seq, V7X_LANES), lambda b, g, tab: (b, g)), nq, t, HEAD_DIM),
        out_shape=jax.ShapeDtypeStruct((batch * seq, MOBA_WIDTH), jnp.bfloat16),
        compiler_params=pltpu.CompilerParams(
            dimension_semantics=("arbitrary", "arbitrary"), vmem_limit_bytes=VMEM_LIMIT),
        name="moba_attn",
    )(tab, mqt, mk, mvt)


def _layer_norm(y, g, b):
    mu = jnp.mean(y, axis=-1, keepdims=True)
    d = y - mu
    var = jnp.mean(d * d, axis=-1, keepdims=True)
    return d * lax.rsqrt(var + LN_EPS) * g + b


def _tail_kernel(a_ref, b_ref, x_ref, wo_ref, g1_ref, b1_ref, win_ref, wout_ref, g2_ref, b2_ref,
                 o_ref, acc_ref):
    mix = jnp.dot(a_ref[...], wo_ref[:DIFF_WIDTH, :], preferred_element_type=jnp.float32)
    mix = mix + jnp.dot(b_ref[...], wo_ref[DIFF_WIDTH:, :], preferred_element_type=jnp.float32)
    x1 = _layer_norm(ALPHA * x_ref[...] + mix, g1_ref[...], b1_ref[...])
    xb = x1.astype(jnp.bfloat16)
    for c in range(D_FF // FFN_CHUNK):
        lo = c * FFN_CHUNK
        gate = jnp.dot(xb, win_ref[:, lo:lo + FFN_CHUNK], preferred_element_type=jnp.float32)
        up = jnp.dot(xb, win_ref[:, D_FF + lo:D_FF + lo + FFN_CHUNK],
                     preferred_element_type=jnp.float32)
        h = (gate * jax.nn.sigmoid(gate) * up).astype(jnp.bfloat16)
        part = jnp.dot(h, wout_ref[lo:lo + FFN_CHUNK, :], preferred_element_type=jnp.float32)
        if c == 0:
            acc_ref[...] = part
        else:
            acc_ref[...] += part
    o_ref[...] = _layer_norm(ALPHA * x1 + acc_ref[...], g2_ref[...], b2_ref[...])


def _tail(a_out, b_out, x2d, wo, g1, b1, win, wout, g2, b2):
    m = x2d.shape[0]
    tm = TAIL_ROWS
    assert D_FF % FFN_CHUNK == 0 and m % tm == 0
    row_tile = lambda w: pl.BlockSpec((tm, w), lambda i: (i, 0))
    whole = lambda a: pl.BlockSpec(a.shape, lambda i: (0, 0), pipeline_mode=pl.Buffered(1))
    return pl.pallas_call(
        _tail_kernel,
        grid=(m // tm,),
        in_specs=[row_tile(DIFF_WIDTH), row_tile(MOBA_WIDTH), row_tile(D_MODEL),
                  whole(wo), whole(g1), whole(b1), whole(win), whole(wout), whole(g2), whole(b2)],
        out_specs=row_tile(D_MODEL),
        out_shape=jax.ShapeDtypeStruct(x2d.shape, jnp.float32),
        scratch_shapes=[pltpu.VMEM((tm, D_MODEL), jnp.float32)],
        compiler_params=pltpu.CompilerParams(dimension_semantics=("arbitrary",),
                                             vmem_limit_bytes=VMEM_LIMIT),
        name="tail",
    )(a_out, b_out, x2d, wo, g1, b1, win, wout, g2, b2)


def _split_w_in(w):
    dk0, dv0, mq0 = DIFF_WIDTH, 2 * DIFF_WIDTH, 3 * DIFF_WIDTH
    mk0, mv0 = mq0 + MOBA_WIDTH, mq0 + 2 * MOBA_WIDTH
    w_k = jnp.concatenate([w[:, dk0:dv0], w[:, mk0:mv0]], axis=1)
    w_t = jnp.concatenate([w[:, :dk0], w[:, mq0:mk0], w[:, dv0:mq0], w[:, mv0:]], axis=1).T
    return w_k.astype(jnp.bfloat16), w_t.astype(jnp.bfloat16)


def kernel(x, w_in, diff_lambda, diff_subln_g, w_o, ln1_g, ln1_b, w_ffn_in, w_ffn_out, ln2_g, ln2_b):
    batch, seq, d = x.shape
    assert d == D_MODEL and seq % ATTN_TILE == 0 and w_in.shape[0] == DEPTH
    x2d = x.reshape(batch * seq, d)
    for l in range(DEPTH):
        w_k, w_t = _split_w_in(w_in[l])
        dk, mk, dqt, mqt, dvt, mvt = _proj(x2d, w_k, w_t, seq)
        a_out = _diff_attn(dqt, dk, dvt, diff_lambda[l], diff_subln_g[l][None, :], batch, seq)
        b_out = _moba_attn(mqt, mk, mvt, batch, seq)
        x2d = _tail(a_out, b_out, x2d, w_o[l].astype(jnp.bfloat16),
                    ln1_g[l][None, :], ln1_b[l][None, :],
                    w_ffn_in[l].astype(jnp.bfloat16), w_ffn_out[l].astype(jnp.bfloat16),
                    ln2_g[l][None, :], ln2_b[l][None, :])
    return x2d.reshape(batch, seq, d)
```

```python
import functools
import math

import numpy as np
import jax
import jax.numpy as jnp
from jax import lax
from jax.experimental import pallas as pl
from jax.experimental.pallas import tpu as pltpu

D_MODEL = 1024
HEAD_DIM = 64
N_DIFF_HEADS = 4
DIFF_WIDTH = N_DIFF_HEADS * 2 * HEAD_DIM
N_MOBA_HEADS = 8
MOBA_WIDTH = N_MOBA_HEADS * HEAD_DIM
ROT_DIM = HEAD_DIM // 4
ROT_HALF = ROT_DIM // 2
ROPE_THETA = 500000.0
MOBA_BLOCK = 256
MOBA_TOPK = 3
D_FF = 2816
DEPTH = 1
ALPHA = (2.0 * DEPTH) ** 0.25
LN_EPS = 1e-5
LAMBDA_INIT = 0.8 - 0.6 * math.exp(-0.3 * 0)
Q_SCALE = HEAD_DIM ** -0.5 * math.log2(math.e)

V7X_LANES = 128
V7X_SUBLANES = 8
V7X_BF16_ROWS = 16
V7X_VMEM_BYTES = 64 * 1024 * 1024
VMEM_LIMIT = V7X_VMEM_BYTES - 8 * 1024 * 1024

DIFF_TILE = 1024
MOBA_TILE = 512
PROJ_ROWS = 512
TAIL_ROWS = 512
FFN_CHUNK = 256
ONES_ROWS = V7X_BF16_ROWS
STEP_UNROLL = 2
NEG_BIG = -1e30

_NT = (((1,), (1,)), ((), ()))


def _rope_tables(seq):
    inv = ROPE_THETA ** (-np.arange(0, ROT_DIM, 2, dtype=np.float64) / ROT_DIM)
    ang = np.arange(seq, dtype=np.float64)[:, None] * inv[None, :]
    cos = np.ones((seq, V7X_LANES), np.float64)
    sin = np.zeros((seq, V7X_LANES), np.float64)
    hot = np.zeros((seq, V7X_LANES), np.float64)
    for base in (0, HEAD_DIM):
        cos[:, base:base + ROT_HALF] = np.cos(ang)
        cos[:, base + ROT_HALF:base + ROT_DIM] = np.cos(ang)
        sin[:, base:base + ROT_HALF] = -np.sin(ang)
        sin[:, base + ROT_HALF:base + ROT_DIM] = np.sin(ang)
    blk = np.arange(seq) // MOBA_BLOCK
    hot[np.arange(seq), blk] = 1.0
    hot[np.arange(seq), HEAD_DIM + blk] = 1.0
    f32 = lambda a: jnp.asarray(a, jnp.float32)
    return f32(cos), f32(sin), f32(hot), f32(np.cos(ang).T), f32(np.sin(ang).T)


def _split_bf16(a):
    hi = a.astype(jnp.bfloat16)
    return hi, (a - hi.astype(jnp.float32)).astype(jnp.bfloat16)


def _proj_kernel(x_ref, wk_ref, wt_ref, cos_ref, sin_ref, hot_ref, cost_ref, sint_ref,
                 dk_ref, mk_ref, dqt_ref, mqt_ref, dvt_ref, mvt_ref, kmean_ref,
                 *, tiles_per_seq):
    tm = x_ref.shape[0]
    t = pl.program_id(0) % tiles_per_seq

    @pl.when(t == 0)
    def _():
        kmean_ref[...] = jnp.zeros_like(kmean_ref)

    xb = x_ref[...].astype(jnp.bfloat16)

    cos = cos_ref[...]
    sin = sin_ref[...]
    lane = lax.broadcasted_iota(jnp.int32, (tm, V7X_LANES), 1)
    first_half = (lane & (HEAD_DIM - 1)) < ROT_HALF

    def rope_rows(a):
        out = []
        for c in range(a.shape[1] // V7X_LANES):
            blk = a[:, c * V7X_LANES:(c + 1) * V7X_LANES]
            partner = jnp.where(first_half,
                                pltpu.roll(blk, V7X_LANES - ROT_HALF, 1),
                                pltpu.roll(blk, ROT_HALF, 1))
            out.append(blk * cos + partner * sin)
        return out

    dk = jnp.dot(xb, wk_ref[:, :DIFF_WIDTH], preferred_element_type=jnp.float32)
    for c, blk in enumerate(rope_rows(dk)):
        dk_ref[:, c * V7X_LANES:(c + 1) * V7X_LANES] = blk.astype(jnp.bfloat16)
    kblocks = rope_rows(jnp.dot(xb, wk_ref[:, DIFF_WIDTH:], preferred_element_type=jnp.float32))

    blocks_per_tile = tm // MOBA_BLOCK
    n_blocks = kmean_ref.shape[0]
    km_row = lax.broadcasted_iota(jnp.int32, (n_blocks, V7X_LANES), 0)
    for c, blk in enumerate(kblocks):
        km_c = kmean_ref[:, c * V7X_LANES:(c + 1) * V7X_LANES]
        for r in range(blocks_per_tile):
            mean = jnp.sum(blk[r * MOBA_BLOCK:(r + 1) * MOBA_BLOCK], axis=0, keepdims=True) \
                * (1.0 / MOBA_BLOCK)
            km_c = jnp.where(km_row == blocks_per_tile * t + r, mean, km_c)
        kmean_ref[:, c * V7X_LANES:(c + 1) * V7X_LANES] = km_c

    hot = hot_ref[...]
    for h in range(N_MOBA_HEADS):
        own_lanes = (lane < HEAD_DIM) if h % 2 == 0 else (lane >= HEAD_DIM)
        k_aug = jnp.where(own_lanes, kblocks[h // 2], hot)
        mk_ref[:, h * V7X_LANES:(h + 1) * V7X_LANES] = k_aug.astype(jnp.bfloat16)

    cost = cost_ref[...]
    sint = sint_ref[...]

    def seg_t(j):
        return lax.dot_general(wt_ref[j * DIFF_WIDTH:(j + 1) * DIFF_WIDTH, :], xb, _NT,
                               preferred_element_type=jnp.float32)

    def rope_t(a):
        x1 = a[:ROT_HALF]
        x2 = a[ROT_HALF:ROT_DIM]
        return jnp.concatenate([x1 * cost - x2 * sint, x2 * cost + x1 * sint, a[ROT_DIM:]],
                               axis=0) * Q_SCALE

    mqt = seg_t(1)
    q_heads = [rope_t(mqt[h * HEAD_DIM:(h + 1) * HEAD_DIM]) for h in range(N_MOBA_HEADS)]

    km = kmean_ref[...]
    km_rows = jnp.concatenate([km] * N_MOBA_HEADS, axis=0)
    r_id = lax.broadcasted_iota(jnp.int32, km_rows.shape, 0)
    l_id = lax.broadcasted_iota(jnp.int32, km_rows.shape, 1)
    km_rows = jnp.where((r_id // n_blocks) == (l_id // HEAD_DIM), km_rows, 0.0)
    k_hi, k_lo = _split_bf16(km_rows)
    q_hi, q_lo = _split_bf16(jnp.concatenate(q_heads, axis=0))
    mm = functools.partial(jnp.dot, preferred_element_type=jnp.float32)
    gate_t = mm(k_hi, q_hi) + mm(k_hi, q_lo) + mm(k_lo, q_hi)

    n_id = lax.broadcasted_iota(jnp.int32, (n_blocks, tm), 0)
    pos = lax.broadcasted_iota(jnp.int32, (n_blocks, tm), 1)
    own = blocks_per_tile * t + pos // MOBA_BLOCK
    past = n_id < own
    zeros = jnp.zeros((HEAD_DIM - n_blocks, tm), jnp.float32)
    for h in range(N_MOBA_HEADS):
        left = jnp.where(past, gate_t[h * n_blocks:(h + 1) * n_blocks], -jnp.inf)
        chosen = jnp.zeros((n_blocks, tm), jnp.int32)
        for _ in range(MOBA_TOPK):
            best = jnp.max(left, axis=0, keepdims=True)
            first = jnp.min(jnp.where(left == best, n_id, n_blocks), axis=0, keepdims=True)
            pick = n_id == first
            chosen = jnp.where(pick, 1, chosen)
            left = jnp.where(pick, -jnp.inf, left)
        bias = jnp.where(n_id == own, 0.0,
                         jnp.where(past & (chosen > 0), 0.0, NEG_BIG))
        parts = [q_heads[h], bias, zeros] if h % 2 == 0 else [bias, zeros, q_heads[h]]
        mqt_ref[0, h] = jnp.concatenate(parts, axis=0).astype(jnp.bfloat16)

    dqt = seg_t(0)
    for h in range(N_DIFF_HEADS):
        maps = [rope_t(dqt[(2 * h + c) * HEAD_DIM:(2 * h + c + 1) * HEAD_DIM]) for c in range(2)]
        dqt_ref[0, h] = jnp.concatenate(maps, axis=0).astype(jnp.bfloat16)
    dvt = seg_t(2)
    for h in range(dvt_ref.shape[1]):
        dvt_ref[0, h] = dvt[h * V7X_LANES:(h + 1) * V7X_LANES].astype(jnp.bfloat16)
    mvt = seg_t(3)
    for g in range(mvt_ref.shape[1]):
        mvt_ref[0, g] = mvt[g * V7X_LANES:(g + 1) * V7X_LANES].astype(jnp.bfloat16)


def _proj(x2d, w_k, w_t, seq):
    m = x2d.shape[0]
    tm = PROJ_ROWS
    tiles_per_seq = seq // tm
    n_blocks = seq // MOBA_BLOCK
    assert n_blocks * N_MOBA_HEADS == V7X_LANES and n_blocks <= HEAD_DIM
    cos, sin, hot, cost, sint = _rope_tables(seq)
    row_tile = lambda w: pl.BlockSpec((tm, w), lambda i: (i, 0))
    whole = lambda a: pl.BlockSpec(a.shape, lambda i: (0, 0))
    table = pl.BlockSpec((tm, V7X_LANES), lambda i: (i % tiles_per_seq, 0))
    table_t = pl.BlockSpec((ROT_HALF, tm), lambda i: (0, i % tiles_per_seq))
    t_tile = lambda n: pl.BlockSpec((1, n, V7X_LANES, tm), lambda i: (i, 0, 0, 0))
    bf = lambda w: jax.ShapeDtypeStruct((m, w), jnp.bfloat16)
    t_shape = lambda n: jax.ShapeDtypeStruct((m // tm, n, V7X_LANES, tm), jnp.bfloat16)
    n_dv = DIFF_WIDTH // V7X_LANES
    n_mv = MOBA_WIDTH // V7X_LANES
    return pl.pallas_call(
        functools.partial(_proj_kernel, tiles_per_seq=tiles_per_seq),
        grid=(m // tm,),
        in_specs=[row_tile(D_MODEL), whole(w_k), whole(w_t), table, table, table,
                  table_t, table_t],
        out_specs=[row_tile(DIFF_WIDTH), row_tile(2 * MOBA_WIDTH),
                   t_tile(N_DIFF_HEADS), t_tile(N_MOBA_HEADS), t_tile(n_dv), t_tile(n_mv)],
        out_shape=[bf(DIFF_WIDTH), bf(2 * MOBA_WIDTH),
                   t_shape(N_DIFF_HEADS), t_shape(N_MOBA_HEADS), t_shape(n_dv), t_shape(n_mv)],
        scratch_shapes=[pltpu.VMEM((n_blocks, MOBA_WIDTH), jnp.float32)],
        compiler_params=pltpu.CompilerParams(dimension_semantics=("arbitrary",),
                                             vmem_limit_bytes=VMEM_LIMIT),
        name="proj",
    )(x2d, w_k, w_t, cos, sin, hot, cost, sint)


def _causal_mask_t(t):
    key = lax.broadcasted_iota(jnp.int32, (t, t), 0)
    qry = lax.broadcasted_iota(jnp.int32, (t, t), 1)
    return key <= qry


def _wide_tile(ref, idx, head, t, rows=slice(None)):
    n = t // ref.shape[-1]
    parts = [ref[idx * n + s, head, rows, :] for s in range(n)]
    return parts[0] if n == 1 else jnp.concatenate(parts, axis=1)


def _with_ones(vt):
    return jnp.concatenate([vt, jnp.ones((ONES_ROWS, vt.shape[1]), vt.dtype)], axis=0)


def _step_table(nq):
    pairs = [(i, i) for i in range(nq)]
    pairs += [(i, j) for j in range(nq - 1) for i in range(j + 1, nq)]
    pairs.append(pairs[-1])
    return jnp.asarray(np.array(pairs, np.int32).T)


def _flash_all_tiles(tab_ref, nq, t, scores, pv, s_ref, p_ref, m_ref, acc_ref, smax_ref):
    n_steps = nq * (nq + 1) // 2
    assert nq % STEP_UNROLL == 0 and (n_steps - nq) % STEP_UNROLL == 0 and STEP_UNROLL % 2 == 0
    mask = _causal_mask_t(t)

    def put_scores(k, slot, masked, x):
        s = scores(tab_ref[0, k], tab_ref[1, k], x)
        if masked is True:
            s = jnp.where(mask, s, -jnp.inf)
        elif masked is not False:
            s = s + jnp.where(mask, 0.0, jnp.where(masked, -jnp.inf, 0.0))
        s_ref[slot, x] = s
        smax_ref[slot, x, 0:1, :] = jnp.max(s, axis=0, keepdims=True)

    def softmax(k, slot, first_tile, x):
        i = tab_ref[0, k]
        m_new = smax_ref[slot, x, 0:1, :]
        if first_tile:
            alpha = jnp.zeros((1, t), jnp.float32)
        else:
            m_old = m_ref[i, x, 0:1, :]
            m_new = jnp.maximum(m_old, m_new)
            alpha = jnp.exp2(m_old - m_new)
        p_ref[slot, x] = jnp.exp2(s_ref[slot, x] - m_new).astype(jnp.bfloat16)
        m_ref[i, x, 0:1, :] = m_new
        return alpha

    def add_values(k, slot, alpha, first_tile, x):
        i = tab_ref[0, k]
        o = pv(tab_ref[1, k], p_ref[slot, x], x)
        acc_ref[i, x] = o if first_tile else alpha * acc_ref[i, x] + o

    def step(k, parity, a_prev, first_tiles, masked_next):
        for x in range(2):
            put_scores(k + 1, 1 - parity, masked_next, x)
        a_cur = [softmax(k, parity, first_tiles, x) for x in range(2)]
        for x in range(2):
            add_values(jnp.maximum(k - 1, 0), 1 - parity, a_prev[x], first_tiles, x)
        return a_cur

    def trip(first, first_tiles, next_masked):
        def body(u, a_prev):
            k0 = first + STEP_UNROLL * u
            for r in range(STEP_UNROLL):
                a_prev = step(k0 + r, r % 2, a_prev, first_tiles, next_masked(k0 + r + 1, r))
            return a_prev
        return body

    acc_ref[nq - 1] = jnp.zeros(acc_ref.shape[1:], acc_ref.dtype)
    p_ref[1] = jnp.zeros(p_ref.shape[1:], p_ref.dtype)
    for x in range(2):
        put_scores(0, 0, True, x)
    zeros = [jnp.zeros((1, t), jnp.float32)] * 2
    diag_next = lambda k, r: True if r < STEP_UNROLL - 1 else k < nq
    a_prev = lax.fori_loop(0, nq // STEP_UNROLL, trip(0, True, diag_next), zeros)
    a_prev = lax.fori_loop(0, (n_steps - nq) // STEP_UNROLL,
                           trip(nq, False, lambda k, r: False), a_prev)
    for x in range(2):
        add_values(n_steps - 1, (n_steps - 1) % 2, a_prev[x], False, x)


def _flash_scratch(nq, t, value_rows):
    return [pltpu.VMEM((2, 2, t, t), jnp.float32), pltpu.VMEM((2, 2, t, t), jnp.bfloat16),
            pltpu.VMEM((nq, 2, V7X_SUBLANES, t), jnp.float32),
            pltpu.VMEM((nq, 2, value_rows + ONES_ROWS, t), jnp.float32),
            pltpu.VMEM((2, 2, V7X_SUBLANES, t), jnp.float32)]


def _flash_grid_spec(grid, in_specs, out_specs, nq, t, value_rows):
    return pltpu.PrefetchScalarGridSpec(
        num_scalar_prefetch=1, grid=grid, in_specs=in_specs, out_specs=out_specs,
        scratch_shapes=_flash_scratch(nq, t, value_rows))


def _diff_kernel(tab_ref, lam_ref, g_ref, qt_ref, k_ref, vt_ref, o_ref,
                 s_ref, p_ref, m_ref, acc_ref, smax_ref):
    t = DIFF_TILE
    nq = k_ref.shape[0] // t
    width = k_ref.shape[1]
    feat = lax.broadcasted_iota(jnp.int32, (width, t), 0)

    def scores(i, j, x):
        qt = _wide_tile(qt_ref, i, 0, t)
        own = (feat < HEAD_DIM) if x == 0 else (feat >= HEAD_DIM)
        k = k_ref[pl.ds(pl.multiple_of(j * t, t), t), :]
        return jnp.dot(k, jnp.where(own, qt, jnp.zeros_like(qt)),
                       preferred_element_type=jnp.float32)

    def pv(j, p, x):
        del x
        return jnp.dot(_with_ones(_wide_tile(vt_ref, j, 0, t)), p,
                       preferred_element_type=jnp.float32)

    _flash_all_tiles(tab_ref, nq, t, scores, pv, s_ref, p_ref, m_ref, acc_ref, smax_ref)

    lp = lam_ref[...]
    lam = (jnp.exp(jnp.sum(lp[0:1] * lp[1:2], axis=1, keepdims=True))
           - jnp.exp(jnp.sum(lp[2:3] * lp[3:4], axis=1, keepdims=True)) + LAMBDA_INIT)
    gain = g_ref[...] * (1.0 - LAMBDA_INIT)

    def finish(i, carry):
        acc1 = acc_ref[i, 0]
        acc2 = acc_ref[i, 1]
        o = (acc1[:width] * (1.0 / acc1[width:width + 1])
             - acc2[:width] * (lam * (1.0 / acc2[width:width + 1])))
        ms = jnp.mean(o * o, axis=0, keepdims=True)
        o = (o * lax.rsqrt(ms + LN_EPS)).T * gain
        o_ref[pl.ds(pl.multiple_of(i * t, t), t), :] = o.astype(o_ref.dtype)
        return carry

    lax.fori_loop(0, nq, finish, 0, unroll=4)


def _diff_attn(dqt, dk, dvt, lam_params, subln_g, batch, seq):
    t = DIFF_TILE
    nq = seq // t
    width = 2 * HEAD_DIM
    tab = _step_table(nq)
    tile_spec = pl.BlockSpec((seq // PROJ_ROWS, 1, width, PROJ_ROWS),
                             lambda b, h, tab: (b, h, 0, 0))
    row_spec = pl.BlockSpec((seq, width), lambda b, h, tab: (b, h))
    whole = lambda a: pl.BlockSpec(a.shape, lambda b, h, tab: (0, 0))
    return pl.pallas_call(
        _diff_kernel,
        grid_spec=_flash_grid_spec(
            (batch, N_DIFF_HEADS),
            [whole(lam_params), whole(subln_g), tile_spec, row_spec, tile_spec],
            row_spec, nq, t, width),
        out_shape=jax.ShapeDtypeStruct(dk.shape, jnp.bfloat16),
        compiler_params=pltpu.CompilerParams(
            dimension_semantics=("arbitrary", "arbitrary"), vmem_limit_bytes=VMEM_LIMIT),
        name="diff_attn",
    )(tab, lam_params, subln_g, dqt, dk, dvt)


def _moba_kernel(tab_ref, qt_ref, k_ref, vt_ref, o_ref, s_ref, p_ref, m_ref, acc_ref, smax_ref):
    t = MOBA_TILE
    nq = k_ref.shape[0] // t

    def scores(i, j, x):
        k = k_ref[pl.ds(pl.multiple_of(j * t, t), t), x * V7X_LANES:(x + 1) * V7X_LANES]
        return jnp.dot(k, _wide_tile(qt_ref, i, x, t), preferred_element_type=jnp.float32)

    def pv(j, p, x):
        vt = _wide_tile(vt_ref, j, 0, t, slice(x * HEAD_DIM, (x + 1) * HEAD_DIM))
        return jnp.dot(_with_ones(vt), p, preferred_element_type=jnp.float32)

    _flash_all_tiles(tab_ref, nq, t, scores, pv, s_ref, p_ref, m_ref, acc_ref, smax_ref)

    def finish(i, carry):
        acca = acc_ref[i, 0]
        accb = acc_ref[i, 1]
        o = jnp.concatenate([acca[:HEAD_DIM] * (1.0 / acca[HEAD_DIM:HEAD_DIM + 1]),
                             accb[:HEAD_DIM] * (1.0 / accb[HEAD_DIM:HEAD_DIM + 1])],
                            axis=0)
        o_ref[pl.ds(pl.multiple_of(i * t, t), t), :] = o.T.astype(o_ref.dtype)
        return carry

    lax.fori_loop(0, nq, finish, 0, unroll=4)


def _moba_attn(mqt, mk, mvt, batch, seq):
    t = MOBA_TILE
    nq = seq // t
    pairs = N_MOBA_HEADS // 2
    tab = _step_table(nq)
    tiles = seq // PROJ_ROWS
    return pl.pallas_call(
        _moba_kernel,
        grid_spec=_flash_grid_spec(
            (batch, pairs),
            [pl.BlockSpec((tiles, 2, V7X_LANES, PROJ_ROWS), lambda b, g, tab: (b, g, 0, 0)),
             pl.BlockSpec((seq, 2 * V7X_LANES), lambda b, g, tab: (b, g)),
             pl.BlockSpec((tiles, 1, V7X_LANES, PROJ_ROWS), lambda b, g, tab: (b, g, 0, 0))],
            pl.BlockSpec((seq, V7X_LANES), lambda b, g, tab: (b, g)), nq, t, HEAD_DIM),
        out_shape=jax.ShapeDtypeStruct((batch * seq, MOBA_WIDTH), jnp.bfloat16),
        compiler_params=pltpu.CompilerParams(
            dimension_semantics=("arbitrary", "arbitrary"), vmem_limit_bytes=VMEM_LIMIT),
        name="moba_attn",
    )(tab, mqt, mk, mvt)


def _layer_norm(y, g, b):
    mu = jnp.mean(y, axis=-1, keepdims=True)
    d = y - mu
    var = jnp.mean(d * d, axis=-1, keepdims=True)
    return d * lax.rsqrt(var + LN_EPS) * g + b


def _tail_kernel(a_ref, b_ref, x_ref, wo_ref, g1_ref, b1_ref, win_ref, wout_ref, g2_ref, b2_ref,
                 o_ref, acc_ref):
    mix = jnp.dot(a_ref[...], wo_ref[:DIFF_WIDTH, :], preferred_element_type=jnp.float32)
    mix = mix + jnp.dot(b_ref[...], wo_ref[DIFF_WIDTH:, :], preferred_element_type=jnp.float32)
    x1 = _layer_norm(ALPHA * x_ref[...] + mix, g1_ref[...], b1_ref[...])
    xb = x1.astype(jnp.bfloat16)
    for c in range(D_FF // FFN_CHUNK):
        lo = c * FFN_CHUNK
        gate = jnp.dot(xb, win_ref[:, lo:lo + FFN_CHUNK], preferred_element_type=jnp.float32)
        up = jnp.dot(xb, win_ref[:, D_FF + lo:D_FF + lo + FFN_CHUNK],
                     preferred_element_type=jnp.float32)
        h = (gate * jax.nn.sigmoid(gate) * up).astype(jnp.bfloat16)
        part = jnp.dot(h, wout_ref[lo:lo + FFN_CHUNK, :], preferred_element_type=jnp.float32)
        if c == 0:
            acc_ref[...] = part
        else:
            acc_ref[...] += part
    o_ref[...] = _layer_norm(ALPHA * x1 + acc_ref[...], g2_ref[...], b2_ref[...])


def _tail(a_out, b_out, x2d, wo, g1, b1, win, wout, g2, b2):
    m = x2d.shape[0]
    tm = TAIL_ROWS
    assert D_FF % FFN_CHUNK == 0 and m % tm == 0
    row_tile = lambda w: pl.BlockSpec((tm, w), lambda i: (i, 0))
    whole = lambda a: pl.BlockSpec(a.shape, lambda i: (0, 0), pipeline_mode=pl.Buffered(1))
    return pl.pallas_call(
        _tail_kernel,
        grid=(m // tm,),
        in_specs=[row_tile(DIFF_WIDTH), row_tile(MOBA_WIDTH), row_tile(D_MODEL),
                  whole(wo), whole(g1), whole(b1), whole(win), whole(wout), whole(g2), whole(b2)],
        out_specs=row_tile(D_MODEL),
        out_shape=jax.ShapeDtypeStruct(x2d.shape, jnp.float32),
        scratch_shapes=[pltpu.VMEM((tm, D_MODEL), jnp.float32)],
        compiler_params=pltpu.CompilerParams(dimension_semantics=("arbitrary",),
                                             vmem_limit_bytes=VMEM_LIMIT),
        name="tail",
    )(a_out, b_out, x2d, wo, g1, b1, win, wout, g2, b2)


def _split_w_in(w):
    dk0, dv0, mq0 = DIFF_WIDTH, 2 * DIFF_WIDTH, 3 * DIFF_WIDTH
    mk0, mv0 = mq0 + MOBA_WIDTH, mq0 + 2 * MOBA_WIDTH
    w_k = jnp.concatenate([w[:, dk0:dv0], w[:, mk0:mv0]], axis=1)
    w_t = jnp.concatenate([w[:, :dk0], w[:, mq0:mk0], w[:, dv0:mq0], w[:, mv0:]], axis=1).T
    return w_k.astype(jnp.bfloat16), w_t.astype(jnp.bfloat16)


def kernel(x, w_in, diff_lambda, diff_subln_g, w_o, ln1_g, ln1_b, w_ffn_in, w_ffn_out, ln2_g, ln2_b):
    batch, seq, d = x.shape
    assert d == D_MODEL and seq % max(DIFF_TILE, MOBA_TILE) == 0 and w_in.shape[0] == DEPTH
    x2d = x.reshape(batch * seq, d)
    for l in range(DEPTH):
        w_k, w_t = _split_w_in(w_in[l])
        dk, mk, dqt, mqt, dvt, mvt = _proj(x2d, w_k, w_t, seq)
        a_out = _diff_attn(dqt, dk, dvt, diff_lambda[l], diff_subln_g[l][None, :], batch, seq)
        b_out = _moba_attn(mqt, mk, mvt, batch, seq)
        x2d = _tail(a_out, b_out, x2d, w_o[l].astype(jnp.bfloat16),
                    ln1_g[l][None, :], ln1_b[l][None, :],
                    w_ffn_in[l].astype(jnp.bfloat16), w_ffn_out[l].astype(jnp.bfloat16),
                    ln2_g[l][None, :], ln2_b[l][None, :])
    return x2d.reshape(batch, seq, d)
```
